```python
import math
import jax, jax.numpy as jnp
from jax import lax
import numpy as np

D_MODEL = 1024
BATCH = 1
SEQ = 16384
DEPTH = 4

HEAD_DIM = 64
NA_HEADS = 8
DIFF_HEADS = 4
GQA_Q_HEADS = 8
GQA_KV_HEADS = 2
BRANCH_WIDTH = 512
N_BRANCHES = 3
D_FF = 2816
GRID_W = 64
NA_WIN_ROWS = 8
NA_WIN_COLS = 16
WINDOW = 128
BLOCK = 128
T5_BUCKETS = 32
T5_MAX_DIST = 128
T5_HEADS = DIFF_HEADS + GQA_Q_HEADS
NEG_INF = -1e30
EPS = 1e-6

PROJ_SPLITS = (
    NA_HEADS * HEAD_DIM, NA_HEADS * HEAD_DIM, NA_HEADS * HEAD_DIM,
    DIFF_HEADS * 2 * HEAD_DIM, DIFF_HEADS * 2 * HEAD_DIM, DIFF_HEADS * 2 * HEAD_DIM,
    GQA_Q_HEADS * HEAD_DIM, GQA_KV_HEADS * HEAD_DIM, GQA_KV_HEADS * HEAD_DIM,
)
W_IN_COLS = sum(PROJ_SPLITS)
PROJ_OFFSETS = tuple(sum(PROJ_SPLITS[:i + 1]) for i in range(len(PROJ_SPLITS) - 1))

kernel_name = "hybrid_parallel_gated_encoder"


def rms_norm(x, g):
    xf = x.astype(jnp.float32)
    y = xf * lax.rsqrt(jnp.mean(xf * xf, axis=-1, keepdims=True) + EPS)
    return (y * g.astype(jnp.float32)).astype(x.dtype)


def swiglu(h, w_gate, w_up, w_down):
    return (jax.nn.silu(h @ w_gate) * (h @ w_up)) @ w_down


def t5_bucket(rel):
    half = T5_BUCKETS // 2
    max_exact = half // 2
    ret = (rel > 0).astype(jnp.int32) * half
    n = jnp.abs(rel)
    nf = jnp.maximum(n, 1).astype(jnp.float32)
    large = max_exact + (jnp.log(nf / max_exact) / math.log(T5_MAX_DIST / max_exact)
                         * (half - max_exact)).astype(jnp.int32)
    large = jnp.minimum(large, half - 1)
    return ret + jnp.where(n < max_exact, n, large)


def neighbourhood_attention(q, k, v, rpb):
    B, S, H, D = q.shape
    rows = S // GRID_W
    wr = min(NA_WIN_ROWS, rows)
    qg = q.reshape(B, rows, GRID_W, H, D)
    kg = k.reshape(B, rows, GRID_W, H, D)
    vg = v.reshape(B, rows, GRID_W, H, D)
    r = jnp.arange(rows)
    rs = jnp.clip(r - wr // 2, 0, rows - wr)
    row_idx = rs[:, None] + jnp.arange(wr)[None, :]
    kn = kg[:, row_idx]
    vn = vg[:, row_idx]
    s = jnp.einsum('brqhd,brikhd->bhrqik', qg, kn) * (D ** -0.5)
    c = jnp.arange(GRID_W)
    cs = jnp.clip(c - NA_WIN_COLS // 2, 0, GRID_W - NA_WIN_COLS)
    col_ok = (c[None, :] >= cs[:, None]) & (c[None, :] < cs[:, None] + NA_WIN_COLS)
    dr = row_idx - r[:, None] + (NA_WIN_ROWS - 1)
    dc = jnp.clip(c[None, :] - c[:, None] + (NA_WIN_COLS - 1), 0, 2 * NA_WIN_COLS - 2)
    bias = rpb[:, dr[:, None, :, None], dc[None, :, None, :]]
    logits = s.astype(jnp.float32) + bias.astype(jnp.float32)[None]
    logits = jnp.where(col_ok[:, None, :], logits, NEG_INF)
    p = jax.nn.softmax(logits.reshape(B, H, rows, GRID_W, wr * GRID_W), axis=-1)
    p = p.reshape(logits.shape).astype(v.dtype)
    o = jnp.einsum('bhrqik,brikhd->brqhd', p, vn)
    return o.reshape(B, S, H * D)


def diff_attention(q, k, v, lam, lam_init, subln_g, bias_table):
    B, S, H, _, D = q.shape
    nb = S // BLOCK
    qb = jnp.moveaxis(q.reshape(B, nb, BLOCK, H, 2, D), 1, 0)
    starts = jnp.arange(nb, dtype=jnp.int32) * BLOCK
    kpos = jnp.arange(S, dtype=jnp.int32)
    qoff = jnp.arange(BLOCK, dtype=jnp.int32)

    def block(args):
        qblk, start = args
        s = jnp.einsum('bqhcd,bkhcd->bhcqk', qblk, k) * (D ** -0.5)
        rel = kpos[None, :] - (start + qoff)[:, None]
        bias = jnp.transpose(bias_table[t5_bucket(rel)], (2, 0, 1))
        p = jax.nn.softmax(s.astype(jnp.float32) + bias.astype(jnp.float32)[None, :, None], axis=-1)
        a = p[:, :, 0] - lam * p[:, :, 1]
        return jnp.einsum('bhqk,bkhe->bqhe', a.astype(v.dtype), v)

    o = lax.map(block, (qb, starts))
    o = jnp.moveaxis(o, 0, 1).reshape(B, S, H, 2 * D)
    o = rms_norm(o, subln_g) * (1.0 - lam_init)
    return o.reshape(B, S, H * 2 * D)


def window_gqa(q, k, v, sink, bias_table):
    B, S, Hq, D = q.shape
    Hkv = k.shape[2]
    G = Hq // Hkv
    nb = S // BLOCK
    pad = ((0, 0), (BLOCK, BLOCK), (0, 0), (0, 0))
    kp = jnp.pad(k, pad).reshape(B, nb + 2, BLOCK, Hkv, D)
    vp = jnp.pad(v, pad).reshape(B, nb + 2, BLOCK, Hkv, D)
    kb = jnp.concatenate([kp[:, :-2], kp[:, 1:-1], kp[:, 2:]], axis=2)
    vb = jnp.concatenate([vp[:, :-2], vp[:, 1:-1], vp[:, 2:]], axis=2)
    qb = q.reshape(B, nb, BLOCK, Hkv, G, D)
    s = jnp.einsum('bnqhgd,bnkhd->bnhgqk', qb, kb) * (D ** -0.5)
    qi = jnp.arange(BLOCK, dtype=jnp.int32)
    kk = jnp.arange(3 * BLOCK, dtype=jnp.int32)
    rel = kk[None, :] - BLOCK - qi[:, None]
    bias = jnp.transpose(bias_table[t5_bucket(rel)], (2, 0, 1)).reshape(Hkv, G, BLOCK, 3 * BLOCK)
    kpos = jnp.arange(nb, dtype=jnp.int32)[:, None] * BLOCK - BLOCK + kk[None, :]
    ok = (jnp.abs(rel) <= WINDOW)[None] & ((kpos >= 0) & (kpos < S))[:, None, :]
    logits = jnp.where(ok[None, :, None, None], s.astype(jnp.float32) + bias.astype(jnp.float32)[None, None], NEG_INF)
    sink_l = jnp.broadcast_to(sink.astype(jnp.float32).reshape(Hkv, G, 1, 1), logits.shape[:-1] + (1,))
    p = jax.nn.softmax(jnp.concatenate([logits, sink_l], axis=-1), axis=-1)[..., :-1]
    o = jnp.einsum('bnhgqk,bnkhd->bnqhgd', p.astype(v.dtype), vb)
    return o.reshape(B, S, Hq * D)


def setup_inputs(seed: int = 0) -> dict:
    key = jax.random.key(seed)
    ks = jax.random.split(key, 16)

    def nrm(k, shape, scale):
        return jax.random.normal(k, shape, jnp.float32) * scale

    return {
        "x": nrm(ks[0], (BATCH, SEQ, D_MODEL), 1.0),
        "w_in": nrm(ks[1], (DEPTH, D_MODEL, W_IN_COLS), D_MODEL ** -0.5),
        "w_branch": nrm(ks[2], (DEPTH, N_BRANCHES, BRANCH_WIDTH, D_MODEL), BRANCH_WIDTH ** -0.5),
        "w_gate": nrm(ks[3], (DEPTH, D_MODEL, N_BRANCHES * D_MODEL), D_MODEL ** -0.5),
        "b_gate": nrm(ks[4], (DEPTH, N_BRANCHES * D_MODEL), 0.02),
        "w_o": nrm(ks[5], (DEPTH, D_MODEL, D_MODEL), D_MODEL ** -0.5),
        "norm_g": 1.0 + nrm(ks[6], (DEPTH, 3, D_MODEL), 0.02),
        "final_g": 1.0 + nrm(ks[7], (D_MODEL,), 0.02),
        "ffn_w_gate": nrm(ks[8], (DEPTH, 2, D_MODEL, D_FF), D_MODEL ** -0.5),
        "ffn_w_up": nrm(ks[9], (DEPTH, 2, D_MODEL, D_FF), D_MODEL ** -0.5),
        "ffn_w_down": nrm(ks[10], (DEPTH, 2, D_FF, D_MODEL), D_FF ** -0.5),
        "na_rpb": nrm(ks[11], (DEPTH, NA_HEADS, 2 * NA_WIN_ROWS - 1, 2 * NA_WIN_COLS - 1), 0.1),
        "diff_lambda": nrm(ks[12], (DEPTH, 4, HEAD_DIM), 0.1),
        "diff_subln_g": 1.0 + nrm(ks[13], (DEPTH, 2 * HEAD_DIM), 0.02),
        "gqa_sink": nrm(ks[14], (DEPTH, GQA_Q_HEADS), 0.5),
        "rel_bias_table": nrm(ks[15], (T5_BUCKETS, T5_HEADS), 0.1),
    }


def reference(x, w_in, w_branch, w_gate, b_gate, w_o, norm_g, final_g,
              ffn_w_gate, ffn_w_up, ffn_w_down, na_rpb, diff_lambda, diff_subln_g,
              gqa_sink, rel_bias_table):
    B, S, _ = x.shape
    for l in range(DEPTH):
        h = rms_norm(x, norm_g[l, 0])
        x = x + 0.5 * swiglu(h, ffn_w_gate[l, 0], ffn_w_up[l, 0], ffn_w_down[l, 0])

        h = rms_norm(x, norm_g[l, 1])
        proj = h @ w_in[l]
        qa, ka, va, qd, kd, vd, qc, kc, vc = jnp.split(proj, PROJ_OFFSETS, axis=-1)

        ya = neighbourhood_attention(qa.reshape(B, S, NA_HEADS, HEAD_DIM),
                                     ka.reshape(B, S, NA_HEADS, HEAD_DIM),
                                     va.reshape(B, S, NA_HEADS, HEAD_DIM), na_rpb[l])

        lam_init = 0.8 - 0.6 * math.exp(-0.3 * l)
        lq = diff_lambda[l].astype(jnp.float32)
        lam = jnp.exp(jnp.sum(lq[0] * lq[1])) - jnp.exp(jnp.sum(lq[2] * lq[3])) + lam_init
        yb = diff_attention(qd.reshape(B, S, DIFF_HEADS, 2, HEAD_DIM),
                            kd.reshape(B, S, DIFF_HEADS, 2, HEAD_DIM),
                            vd.reshape(B, S, DIFF_HEADS, 2 * HEAD_DIM),
                            lam, lam_init, diff_subln_g[l], rel_bias_table[:, :DIFF_HEADS])

        yc = window_gqa(qc.reshape(B, S, GQA_Q_HEADS, HEAD_DIM),
                        kc.reshape(B, S, GQA_KV_HEADS, HEAD_DIM),
                        vc.reshape(B, S, GQA_KV_HEADS, HEAD_DIM),
                        gqa_sink[l], rel_bias_table[:, DIFF_HEADS:])

        g = jax.nn.sigmoid(h @ w_gate[l] + b_gate[l]).reshape(B, S, N_BRANCHES, D_MODEL)
        merged = (g[:, :, 0] * (ya @ w_branch[l, 0])
                  + g[:, :, 1] * (yb @ w_branch[l, 1])
                  + g[:, :, 2] * (yc @ w_branch[l, 2]))
        x = x + merged @ w_o[l]

        h = rms_norm(x, norm_g[l, 2])
        x = x + 0.5 * swiglu(h, ffn_w_gate[l, 1], ffn_w_up[l, 1], ffn_w_down[l, 1])
    return rms_norm(x, final_g)
```

```python
import functools
import math

import jax
import jax.numpy as jnp
import numpy as np
from jax import lax
from jax.experimental import pallas as pl
from jax.experimental.pallas import tpu as pltpu

D_MODEL = 1024
DEPTH = 4
HEAD_DIM = 64
NA_HEADS = 8
DIFF_HEADS = 4
GQA_Q_HEADS = 8
GQA_KV_HEADS = 2
GQA_GROUP = GQA_Q_HEADS // GQA_KV_HEADS
BRANCH_WIDTH = 512
N_BRANCHES = 3
D_FF = 2816
GRID_W = 64
NA_WIN_ROWS = 8
NA_WIN_COLS = 16
WINDOW = 128
T5_BUCKETS = 32
T5_MAX_DIST = 128
NEG_INF = -1e30
EPS = 1e-6

QA_OFF, KA_OFF, VA_OFF = 0, 512, 1024
QD_OFF, KD_OFF, VD_OFF = 1536, 2048, 2560
QC_OFF, KC_OFF, VC_OFF = 3072, 3584, 3712
W_IN_COLS = 3840

LANES = 128
VMEM_LIMIT = 56 * 1024 * 1024

ROW_TILE = 512
FFN_ROW_TILE = 256
NA_BLOCK_ROWS = 8
NA_WIN_BLOCK_ROWS = 16
DIFF_TQ = 512
DIFF_TK = 512
GQA_QB = 512
GQA_KW = GQA_QB + 2 * WINDOW

F32 = jnp.float32
BF16 = jnp.bfloat16


def _resident(shape, index_map):
    return pl.BlockSpec(shape, index_map, pipeline_mode=pl.Buffered(1))


def _params(*sem):
    return pltpu.CompilerParams(dimension_semantics=sem, vmem_limit_bytes=VMEM_LIMIT)


def _rms(x, g):
    return x * lax.rsqrt(jnp.mean(x * x, axis=-1, keepdims=True) + EPS) * g


def _sigmoid(z):
    return 1.0 / (1.0 + jnp.exp(-z))


def _ffn_kernel(x_ref, g_ref, wg_ref, wu_ref, wd_ref, fg_ref, o_ref, *, final):
    x = x_ref[...]
    h = _rms(x, g_ref[...]).astype(BF16)
    gate = jnp.dot(h, wg_ref[...], preferred_element_type=F32)
    up = jnp.dot(h, wu_ref[...], preferred_element_type=F32)
    a = (gate * _sigmoid(gate) * up).astype(BF16)
    y = x + 0.5 * jnp.dot(a, wd_ref[...], preferred_element_type=F32)
    if final:
        y = _rms(y, fg_ref[...])
    o_ref[...] = y


def _ffn(x, g, wg, wu, wd, fg, final):
    S = x.shape[0]
    tm = FFN_ROW_TILE
    return pl.pallas_call(
        functools.partial(_ffn_kernel, final=final),
        grid=(S // tm,),
        in_specs=[
            pl.BlockSpec((tm, D_MODEL), lambda i: (i, 0)),
            _resident((1, D_MODEL), lambda i: (0, 0)),
            _resident((D_MODEL, D_FF), lambda i: (0, 0)),
            _resident((D_MODEL, D_FF), lambda i: (0, 0)),
            _resident((D_FF, D_MODEL), lambda i: (0, 0)),
            _resident((1, D_MODEL), lambda i: (0, 0)),
        ],
        out_specs=pl.BlockSpec((tm, D_MODEL), lambda i: (i, 0)),
        out_shape=jax.ShapeDtypeStruct((S, D_MODEL), F32),
        compiler_params=_params("parallel"),
        name="ffn",
    )(x, g, wg, wu, wd, fg)


def _proj_kernel(x_ref, g_ref, w_ref, o_ref):
    h = _rms(x_ref[...], g_ref[...]).astype(BF16)
    o_ref[...] = jnp.dot(h, w_ref[...], preferred_element_type=F32).astype(BF16)


def _proj(x, g, w):
    S = x.shape[0]
    tm = ROW_TILE
    return pl.pallas_call(
        _proj_kernel,
        grid=(S // tm,),
        in_specs=[
            pl.BlockSpec((tm, D_MODEL), lambda i: (i, 0)),
            _resident((1, D_MODEL), lambda i: (0, 0)),
            _resident((D_MODEL, W_IN_COLS), lambda i: (0, 0)),
        ],
        out_specs=pl.BlockSpec((tm, W_IN_COLS), lambda i: (i, 0)),
        out_shape=jax.ShapeDtypeStruct((S, W_IN_COLS), BF16),
        compiler_params=_params("parallel"),
        name="proj",
    )(x, g, w)


def _na_kernel(q_ref, k_ref, v_ref, b_ref, o_ref, *, rows):
    blk = pl.program_id(1)
    nq = NA_BLOCK_ROWS * GRID_W
    nk = NA_WIN_BLOCK_ROWS * GRID_W
    start_row = jnp.clip(blk * NA_BLOCK_ROWS - NA_WIN_ROWS // 2, 0, rows - NA_WIN_BLOCK_ROWS)
    start = pl.multiple_of(start_row * GRID_W, 4 * GRID_W)
    q = q_ref[...].astype(F32)
    kb = k_ref[pl.ds(start, nk), :]
    vb = v_ref[pl.ds(start, nk), :]
    lane = lax.broadcasted_iota(jnp.int32, (nq, LANES), 1)
    first = lane < HEAD_DIM
    outs = []
    for hh in range(2):
        qm = jnp.where(first if hh == 0 else jnp.logical_not(first), q, 0.0).astype(BF16)
        s = lax.dot_general(qm, kb, (((1,), (1,)), ((), ())), preferred_element_type=F32)
        s = s + b_ref[0, hh]
        m = jnp.max(s, axis=-1, keepdims=True)
        p = jnp.exp(s - m)
        l = jnp.sum(p, axis=-1, keepdims=True)
        o = jnp.dot(p.astype(BF16), vb, preferred_element_type=F32)
        outs.append(o / l)
    o_ref[...] = jnp.where(first, outs[0], outs[1]).astype(o_ref.dtype)


def _na(proj, bias):
    S = proj.shape[0]
    rows = S // GRID_W
    nq = NA_BLOCK_ROWS * GRID_W
    nk = NA_WIN_BLOCK_ROWS * GRID_W
    nb = S // nq
    qcol, kcol, vcol = QA_OFF // LANES, KA_OFF // LANES, VA_OFF // LANES

    def bias_map(hp, b):
        return (jnp.where(b == 0, 0, jnp.where(b == nb - 1, 2, 1)), hp, 0, 0)

    return pl.pallas_call(
        functools.partial(_na_kernel, rows=rows),
        grid=(NA_HEADS // 2, nb),
        in_specs=[
            pl.BlockSpec((nq, LANES), lambda hp, b: (b, qcol + hp)),
            pl.BlockSpec((S, LANES), lambda hp, b: (0, kcol + hp)),
            pl.BlockSpec((S, LANES), lambda hp, b: (0, vcol + hp)),
            pl.BlockSpec((1, 2, nq, nk), bias_map),
        ],
        out_specs=pl.BlockSpec((nq, LANES), lambda hp, b: (b, hp)),
        out_shape=jax.ShapeDtypeStruct((S, NA_HEADS * HEAD_DIM), BF16),
        compiler_params=_params("parallel", "arbitrary"),
        name="na_attn",
    )(proj, proj, proj, bias)


def _na_bias_tables(rpb, rows):
    br, wr = NA_BLOCK_ROWS, NA_WIN_BLOCK_ROWS
    c = np.arange(GRID_W)
    cs = np.clip(c - NA_WIN_COLS // 2, 0, GRID_W - NA_WIN_COLS)
    col_ok = (c[None, :] >= cs[:, None]) & (c[None, :] < cs[:, None] + NA_WIN_COLS)
    dc = np.clip(c[None, :] - c[:, None] + (NA_WIN_COLS - 1), 0, 2 * NA_WIN_COLS - 2)
    blocks = (0, 1, rows // br - 1)
    dr = np.zeros((3, br, wr), np.int32)
    ok = np.zeros((3, br, wr), bool)
    for t, b in enumerate(blocks):
        win0 = int(np.clip(b * br - NA_WIN_ROWS // 2, 0, rows - wr))
        for j in range(br):
            r = b * br + j
            rs = int(np.clip(r - NA_WIN_ROWS // 2, 0, rows - NA_WIN_ROWS))
            for w in range(wr):
                row = win0 + w
                if rs <= row < rs + NA_WIN_ROWS:
                    ok[t, j, w] = True
                    dr[t, j, w] = row - r + (NA_WIN_ROWS - 1)
    t_all = rpb.astype(F32)[:, :, :, dc]
    t_all = jnp.where(col_ok[None, None, None], t_all, NEG_INF)
    big = t_all[:, :, dr]
    big = jnp.where(ok[None, None, :, :, :, None, None], big, NEG_INF)
    big = jnp.transpose(big, (0, 2, 1, 3, 5, 4, 6))
    L, H = rpb.shape[0], rpb.shape[1]
    return big.reshape(L, 3, H, br * GRID_W, wr * GRID_W)


def _t5_bucket(rel):
    half = T5_BUCKETS // 2
    max_exact = half // 2
    ret = (rel > 0).astype(jnp.int32) * half
    n = jnp.abs(rel)
    nf = jnp.maximum(n, 1).astype(jnp.float32)
    large = max_exact + (jnp.log(nf / max_exact) / math.log(T5_MAX_DIST / max_exact)
                         * (half - max_exact)).astype(jnp.int32)
    large = jnp.minimum(large, half - 1)
    return ret + jnp.where(n < max_exact, n, large)


def _toeplitz(vals, n_rows, n_cols):
    length = n_rows + n_cols - 1
    lead = vals.shape[:-1]
    w = jnp.concatenate([vals, jnp.zeros(lead + (1,), vals.dtype)], axis=-1)
    flat = jnp.tile(w, (1,) * len(lead) + (n_rows,))[..., : n_rows * length]
    return flat.reshape(lead + (n_rows, length))[..., n_rows - 1:]


def _rel_bias(table, lo, hi):
    rel = jnp.arange(lo, hi + 1, dtype=jnp.int32)
    return jnp.transpose(table.astype(F32)[_t5_bucket(rel)], (1, 0))


def _diff_kernel(scal_ref, q_ref, k_ref, v_ref, b_ref, g_ref, lq_ref, o_ref,
                 qa_ref, qb_ref, m_ref, l_ref, acc_ref, *, nk):
    h = pl.program_id(0)
    qi = pl.program_id(1)
    tq, tk = DIFF_TQ, DIFF_TK
    two_d = 2 * HEAD_DIM

    q = q_ref[0].astype(F32)
    row = lax.broadcasted_iota(jnp.int32, (two_d, tq), 0)
    qa_ref[...] = jnp.where(row < HEAD_DIM, q, 0.0).astype(BF16)
    qb_ref[...] = jnp.where(row >= HEAD_DIM, q, 0.0).astype(BF16)
    m_ref[...] = jnp.full(m_ref.shape, NEG_INF, F32)
    l_ref[...] = jnp.zeros(l_ref.shape, F32)
    acc_ref[...] = jnp.zeros(acc_ref.shape, F32)

    def tile(ki, bias, shift):
        kb = k_ref[pl.ds(pl.multiple_of(ki * tk, tk), tk), :]
        vt = v_ref[0, ki]
        for c, qm_ref in enumerate((qa_ref, qb_ref)):
            s = jnp.dot(kb, qm_ref[...], preferred_element_type=F32)
            if bias is not None:
                s = s + bias
            m_old = m_ref[c]
            m_new = jnp.maximum(m_old, jnp.max(s, axis=0, keepdims=True) + shift)
            alpha = jnp.exp(m_old - m_new)
            p = jnp.exp(s - (m_new - shift))
            l_ref[c] = alpha * l_ref[c] + jnp.sum(p, axis=0, keepdims=True)
            acc_ref[c] = alpha * acc_ref[c] + jnp.dot(vt, p.astype(BF16), preferred_element_type=F32)
            m_ref[c] = m_new

    c_left = scal_ref[h]
    c_right = scal_ref[DIFF_HEADS + h]

    def far_left(ki, carry):
        tile(ki, None, c_left)
        return carry

    def far_right(ki, carry):
        tile(ki, None, c_right)
        return carry

    lax.fori_loop(0, jnp.maximum(qi - 1, 0), far_left, 0)

    @pl.when(qi >= 1)
    def _():
        tile(qi - 1, b_ref[0, 0], 0.0)

    tile(qi, b_ref[0, 1], 0.0)

    @pl.when(qi + 1 < nk)
    def _():
        tile(qi + 1, b_ref[0, 2], 0.0)

    lax.fori_loop(jnp.minimum(qi + 2, nk), nk, far_right, 0)

    lam_init = scal_ref[2 * DIFF_HEADS]
    lq = lq_ref[...]
    lam = (jnp.exp(jnp.sum(lq[0:1] * lq[1:2], axis=-1, keepdims=True))
           - jnp.exp(jnp.sum(lq[2:3] * lq[3:4], axis=-1, keepdims=True)) + lam_init)
    o = acc_ref[0] / l_ref[0] - lam * (acc_ref[1] / l_ref[1])
    y = o * lax.rsqrt(jnp.mean(o * o, axis=0, keepdims=True) + EPS) * g_ref[...]
    o_ref[0] = (y * (1.0 - lam_init)).astype(o_ref.dtype)


def _diff(scal, q_t, proj, v_t, bias_t, subln_g, lq):
    S = proj.shape[0]
    tq, tk = DIFF_TQ, DIFF_TK
    nk = S // tk
    two_d = 2 * HEAD_DIM
    kcol = KD_OFF // LANES
    return pl.pallas_call(
        functools.partial(_diff_kernel, nk=nk),
        grid=(DIFF_HEADS, S // tq),
        in_specs=[
            pl.BlockSpec(memory_space=pltpu.SMEM),
            pl.BlockSpec((1, two_d, tq), lambda h, i: (h, 0, i)),
            pl.BlockSpec((S, LANES), lambda h, i: (0, kcol + h)),
            pl.BlockSpec((1, nk, two_d, tk), lambda h, i: (h, 0, 0, 0)),
            pl.BlockSpec((1, 3, tk, tq), lambda h, i: (h, 0, 0, 0)),
            _resident((two_d, 1), lambda h, i: (0, 0)),
            _resident((4, HEAD_DIM), lambda h, i: (0, 0)),
        ],
        out_specs=pl.BlockSpec((1, two_d, tq), lambda h, i: (h, 0, i)),
        out_shape=jax.ShapeDtypeStruct((DIFF_HEADS, two_d, S), BF16),
        scratch_shapes=[
            pltpu.VMEM((two_d, tq), BF16),
            pltpu.VMEM((two_d, tq), BF16),
            pltpu.VMEM((2, 1, tq), F32),
            pltpu.VMEM((2, 1, tq), F32),
            pltpu.VMEM((2, two_d, tq), F32),
        ],
        compiler_params=_params("parallel", "arbitrary"),
        name="diff_attn",
    )(scal, q_t, proj, v_t, bias_t, subln_g, lq)


def _diff_bias_tiles(table):
    tq, tk = DIFF_TQ, DIFF_TK
    tiles = []
    for d in (-1, 0, 1):
        vals = _rel_bias(table, d * tk - (tq - 1), d * tk + tk - 1)
        tiles.append(jnp.swapaxes(_toeplitz(vals, tq, tk), -1, -2))
    return jnp.stack(tiles, axis=1)


def _gqa_kernel(q_ref, kp_ref, kc_ref, kn_ref, vp_ref, vc_ref, vn_ref, b_ref, sink_ref, o_ref, *, seq):
    blk = pl.program_id(1)
    qb, kw = GQA_QB, GQA_KW
    width = GQA_GROUP * HEAD_DIM
    q = q_ref[...].astype(F32)
    lane = lax.broadcasted_iota(jnp.int32, (qb, width), 1) // HEAD_DIM
    qs = jnp.concatenate(
        [jnp.where(lane == g, q, 0.0).astype(BF16) for g in range(GQA_GROUP)], axis=0)
    kx = jnp.concatenate([kp_ref[...], kc_ref[...], kn_ref[...]], axis=0)
    vx = jnp.concatenate([vp_ref[...], vc_ref[...], vn_ref[...]], axis=0)
    s = lax.dot_general(qs, kx, (((1,), (1,)), ((), ())), preferred_element_type=F32)
    s = s + b_ref[0]
    kpos = blk * qb - WINDOW + lax.broadcasted_iota(jnp.int32, (1, kw), 1)
    s = jnp.where((kpos >= 0) & (kpos < seq), s, NEG_INF)
    sink = sink_ref[0]
    m = jnp.maximum(jnp.max(s, axis=-1, keepdims=True), sink)
    p = jnp.exp(s - m)
    l = jnp.sum(p, axis=-1, keepdims=True) + jnp.exp(sink - m)
    o = jnp.dot(p.astype(BF16), vx, preferred_element_type=F32) / l
    out = jnp.zeros((qb, width), F32)
    for g in range(GQA_GROUP):
        out = jnp.where(lane == g, o[g * qb:(g + 1) * qb], out)
    o_ref[...] = out.astype(o_ref.dtype)


def _gqa(proj, kx, vx, bias, sink_rows):
    S = proj.shape[0]
    qb, kw = GQA_QB, GQA_KW
    width = GQA_GROUP * HEAD_DIM
    nb = S // qb
    sub = qb // WINDOW
    last = S // WINDOW - 1
    qcol = QC_OFF // width

    def prev_map(kv, b):
        return (jnp.maximum(b * sub - 1, 0), kv)

    def next_map(kv, b):
        return (jnp.minimum(b * sub + sub, last), kv)

    side = pl.BlockSpec((WINDOW, width), prev_map)
    nxt = pl.BlockSpec((WINDOW, width), next_map)
    cur = pl.BlockSpec((qb, width), lambda kv, b: (b, kv))
    return pl.pallas_call(
        functools.partial(_gqa_kernel, seq=S),
        grid=(GQA_KV_HEADS, nb),
        in_specs=[
            pl.BlockSpec((qb, width), lambda kv, b: (b, qcol + kv)),
            side, cur, nxt, side, cur, nxt,
            pl.BlockSpec((1, GQA_GROUP * qb, kw), lambda kv, b: (kv, 0, 0)),
            pl.BlockSpec((1, GQA_GROUP * qb, 1), lambda kv, b: (kv, 0, 0)),
        ],
        out_specs=pl.BlockSpec((qb, width), lambda kv, b: (b, kv)),
        out_shape=jax.ShapeDtypeStruct((S, GQA_Q_HEADS * HEAD_DIM), BF16),
        compiler_params=_params("parallel", "arbitrary"),
        name="gqa_attn",
    )(proj, kx, kx, kx, vx, vx, vx, bias, sink_rows)


def _gqa_bias(table):
    qb, kw = GQA_QB, GQA_KW
    vals = _rel_bias(table, -WINDOW - (qb - 1), kw - 1 - WINDOW)
    rel = jnp.arange(-WINDOW - (qb - 1), kw - WINDOW, dtype=jnp.int32)
    vals = jnp.where(jnp.abs(rel)[None, :] <= WINDOW, vals, NEG_INF)
    bias = _toeplitz(vals, qb, kw)
    return bias.reshape(GQA_KV_HEADS, GQA_GROUP * qb, kw)


def _merge_kernel(x_ref, ya_ref, yb_ref, yc_ref, g_ref, wg_ref, bg_ref, wb_ref, wo_ref, o_ref):
    x = x_ref[...]
    h = _rms(x, g_ref[...]).astype(BF16)
    gates = _sigmoid(jnp.dot(h, wg_ref[...], preferred_element_type=F32) + bg_ref[...])
    merged = None
    for i, y_ref in enumerate((ya_ref, yb_ref, yc_ref)):
        t = gates[:, i * D_MODEL:(i + 1) * D_MODEL] * jnp.dot(y_ref[...], wb_ref[i], preferred_element_type=F32)
        merged = t if merged is None else merged + t
    o_ref[...] = x + jnp.dot(merged.astype(BF16), wo_ref[...], preferred_element_type=F32)


def _merge(x, ya, yb, yc, g, wg, bg, wb, wo):
    S = x.shape[0]
    tm = ROW_TILE
    branch = pl.BlockSpec((tm, BRANCH_WIDTH), lambda i: (i, 0))
    return pl.pallas_call(
        _merge_kernel,
        grid=(S // tm,),
        in_specs=[
            pl.BlockSpec((tm, D_MODEL), lambda i: (i, 0)),
            branch, branch, branch,
            _resident((1, D_MODEL), lambda i: (0, 0)),
            _resident((D_MODEL, N_BRANCHES * D_MODEL), lambda i: (0, 0)),
            _resident((1, N_BRANCHES * D_MODEL), lambda i: (0, 0)),
            _resident((N_BRANCHES, BRANCH_WIDTH, D_MODEL), lambda i: (0, 0, 0)),
            _resident((D_MODEL, D_MODEL), lambda i: (0, 0)),
        ],
        out_specs=pl.BlockSpec((tm, D_MODEL), lambda i: (i, 0)),
        out_shape=jax.ShapeDtypeStruct((S, D_MODEL), F32),
        compiler_params=_params("parallel"),
        name="merge",
    )(x, ya, yb, yc, g, wg, bg, wb, wo)


def kernel(x, w_in, w_branch, w_gate, b_gate, w_o, norm_g, final_g, ffn_w_gate, ffn_w_up, ffn_w_down,
           na_rpb, diff_lambda, diff_subln_g, gqa_sink, rel_bias_table):
    B, S, _ = x.shape
    assert B == 1 and S % (NA_WIN_BLOCK_ROWS * GRID_W) == 0 and S // (NA_BLOCK_ROWS * GRID_W) >= 3
    depth = w_in.shape[0]
    two_d = 2 * HEAD_DIM
    nk = S // DIFF_TK

    col_scale = np.ones((W_IN_COLS,), np.float32)
    for off, width in ((QA_OFF, 512), (QD_OFF, 512), (QC_OFF, 512)):
        col_scale[off:off + width] = HEAD_DIM ** -0.5
    w_in_b = (w_in * col_scale).astype(BF16)
    w_branch_b = w_branch.astype(BF16)
    w_gate_b = w_gate.astype(BF16)
    w_o_b = w_o.astype(BF16)
    ffn_wg_b = ffn_w_gate.astype(BF16)
    ffn_wu_b = ffn_w_up.astype(BF16)
    ffn_wd_b = ffn_w_down.astype(BF16)

    na_bias = _na_bias_tables(na_rpb, S // GRID_W)
    diff_table = rel_bias_table[:, :DIFF_HEADS]
    diff_bias = _diff_bias_tiles(diff_table)
    far = _rel_bias(diff_table, -T5_MAX_DIST, T5_MAX_DIST)
    gqa_bias = _gqa_bias(rel_bias_table[:, DIFF_HEADS:])
    final_row = final_g.reshape(1, D_MODEL).astype(F32)

    xs = x.reshape(S, D_MODEL)
    for l in range(depth):
        xs = _ffn(xs, norm_g[l, 0].reshape(1, D_MODEL), ffn_wg_b[l, 0], ffn_wu_b[l, 0], ffn_wd_b[l, 0],
                  final_row, False)
        g1 = norm_g[l, 1].reshape(1, D_MODEL)
        proj = _proj(xs, g1, w_in_b[l])

        ya = _na(proj, na_bias[l])

        lam_init = 0.8 - 0.6 * math.exp(-0.3 * l)
        scal = jnp.concatenate([far[:, 0], far[:, -1], jnp.full((1,), lam_init, F32)])
        q_t = proj[:, QD_OFF:QD_OFF + 512].T.reshape(DIFF_HEADS, two_d, S)
        v_t = proj[:, VD_OFF:VD_OFF + 512].T.reshape(DIFF_HEADS, two_d, nk, DIFF_TK).transpose(0, 2, 1, 3)
        yb_t = _diff(scal, q_t, proj, v_t, diff_bias, diff_subln_g[l].reshape(two_d, 1).astype(F32),
                     diff_lambda[l].astype(F32))
        yb = yb_t.reshape(DIFF_HEADS * two_d, S).T

        kx = jnp.tile(proj[:, KC_OFF:KC_OFF + 128].reshape(S, GQA_KV_HEADS, 1, HEAD_DIM),
                      (1, 1, GQA_GROUP, 1)).reshape(S, GQA_Q_HEADS * HEAD_DIM)
        vx = jnp.tile(proj[:, VC_OFF:VC_OFF + 128].reshape(S, GQA_KV_HEADS, 1, HEAD_DIM),
                      (1, 1, GQA_GROUP, 1)).reshape(S, GQA_Q_HEADS * HEAD_DIM)
        sink_rows = jnp.repeat(gqa_sink[l].astype(F32), GQA_QB).reshape(GQA_KV_HEADS, GQA_GROUP * GQA_QB, 1)
        yc = _gqa(proj, kx, vx, gqa_bias, sink_rows)

        xs = _merge(xs, ya, yb, yc, g1, w_gate_b[l], b_gate[l].reshape(1, -1).astype(F32),
                    w_branch_b[l], w_o_b[l])
        xs = _ffn(xs, norm_g[l, 2].reshape(1, D_MODEL), ffn_wg_b[l, 1], ffn_wu_b[l, 1], ffn_wd_b[l, 1],
                  final_row, l == depth - 1)
    return xs.reshape(B, S, D_MODEL)
```

```python
import functools
import math

import jax
import jax.numpy as jnp
import numpy as np
from jax import lax
from jax.experimental import pallas as pl
from jax.experimental.pallas import tpu as pltpu

D_MODEL = 1024
DEPTH = 4
HEAD_DIM = 64
NA_HEADS = 8
DIFF_HEADS = 4
GQA_Q_HEADS = 8
GQA_KV_HEADS = 2
GQA_GROUP = GQA_Q_HEADS // GQA_KV_HEADS
BRANCH_WIDTH = 512
N_BRANCHES = 3
D_FF = 2816
GRID_W = 64
NA_WIN_ROWS = 8
NA_WIN_COLS = 16
WINDOW = 128
T5_BUCKETS = 32
T5_MAX_DIST = 128
NEG_INF = -1e30
EPS = 1e-6

QA_OFF, KA_OFF, VA_OFF = 0, 512, 1024
QD_OFF, KD_OFF, VD_OFF = 1536, 2048, 2560
QC_OFF, KC_OFF, VC_OFF = 3072, 3584, 3712
W_IN_COLS = 3840

LANES = 128
VMEM_LIMIT = 56 * 1024 * 1024

ROW_TILE = 512
FFN_ROW_TILE = 256
NA_BLOCK_ROWS = 8
NA_WIN_BLOCK_ROWS = 16
DIFF_TQ = 512
DIFF_TK = 512
DIFF_CHUNK = 256
DIFF_VROWS = 2 * HEAD_DIM + 16
LOG2E = math.log2(math.e)
GQA_QB = 512
GQA_KW = GQA_QB + 2 * WINDOW

F32 = jnp.float32
BF16 = jnp.bfloat16


def _resident(shape, index_map):
    return pl.BlockSpec(shape, index_map, pipeline_mode=pl.Buffered(1))


def _params(*sem):
    return pltpu.CompilerParams(dimension_semantics=sem, vmem_limit_bytes=VMEM_LIMIT)


def _rms(x, g):
    return x * lax.rsqrt(jnp.mean(x * x, axis=-1, keepdims=True) + EPS) * g


def _sigmoid(z):
    return 1.0 / (1.0 + jnp.exp(-z))


def _ffn_kernel(x_ref, g_ref, wg_ref, wu_ref, wd_ref, fg_ref, o_ref, *, final):
    x = x_ref[...]
    h = _rms(x, g_ref[...]).astype(BF16)
    gate = jnp.dot(h, wg_ref[...], preferred_element_type=F32)
    up = jnp.dot(h, wu_ref[...], preferred_element_type=F32)
    a = (gate * _sigmoid(gate) * up).astype(BF16)
    y = x + 0.5 * jnp.dot(a, wd_ref[...], preferred_element_type=F32)
    if final:
        y = _rms(y, fg_ref[...])
    o_ref[...] = y


def _ffn(x, g, wg, wu, wd, fg, final):
    S = x.shape[0]
    tm = FFN_ROW_TILE
    return pl.pallas_call(
        functools.partial(_ffn_kernel, final=final),
        grid=(S // tm,),
        in_specs=[
            pl.BlockSpec((tm, D_MODEL), lambda i: (i, 0)),
            _resident((1, D_MODEL), lambda i: (0, 0)),
            _resident((D_MODEL, D_FF), lambda i: (0, 0)),
            _resident((D_MODEL, D_FF), lambda i: (0, 0)),
            _resident((D_FF, D_MODEL), lambda i: (0, 0)),
            _resident((1, D_MODEL), lambda i: (0, 0)),
        ],
        out_specs=pl.BlockSpec((tm, D_MODEL), lambda i: (i, 0)),
        out_shape=jax.ShapeDtypeStruct((S, D_MODEL), F32),
        compiler_params=_params("parallel"),
        name="ffn",
    )(x, g, wg, wu, wd, fg)


def _proj_kernel(x_ref, g_ref, w_ref, o_ref):
    h = _rms(x_ref[...], g_ref[...]).astype(BF16)
    o_ref[...] = jnp.dot(h, w_ref[...], preferred_element_type=F32).astype(BF16)


def _proj(x, g, w):
    S = x.shape[0]
    tm = ROW_TILE
    return pl.pallas_call(
        _proj_kernel,
        grid=(S // tm,),
        in_specs=[
            pl.BlockSpec((tm, D_MODEL), lambda i: (i, 0)),
            _resident((1, D_MODEL), lambda i: (0, 0)),
            _resident((D_MODEL, W_IN_COLS), lambda i: (0, 0)),
        ],
        out_specs=pl.BlockSpec((tm, W_IN_COLS), lambda i: (i, 0)),
        out_shape=jax.ShapeDtypeStruct((S, W_IN_COLS), BF16),
        compiler_params=_params("parallel"),
        name="proj",
    )(x, g, w)


def _na_kernel(q_ref, k_ref, v_ref, b_ref, o_ref, *, rows):
    blk = pl.program_id(1)
    nq = NA_BLOCK_ROWS * GRID_W
    nk = NA_WIN_BLOCK_ROWS * GRID_W
    start_row = jnp.clip(blk * NA_BLOCK_ROWS - NA_WIN_ROWS // 2, 0, rows - NA_WIN_BLOCK_ROWS)
    start = pl.multiple_of(start_row * GRID_W, 4 * GRID_W)
    q = q_ref[...].astype(F32)
    kb = k_ref[pl.ds(start, nk), :]
    vb = v_ref[pl.ds(start, nk), :]
    lane = lax.broadcasted_iota(jnp.int32, (nq, LANES), 1)
    first = lane < HEAD_DIM
    outs = []
    for hh in range(2):
        qm = jnp.where(first if hh == 0 else jnp.logical_not(first), q, 0.0).astype(BF16)
        s = lax.dot_general(qm, kb, (((1,), (1,)), ((), ())), preferred_element_type=F32)
        s = s + b_ref[0, hh]
        m = jnp.max(s, axis=-1, keepdims=True)
        p = jnp.exp(s - m)
        l = jnp.sum(p, axis=-1, keepdims=True)
        o = jnp.dot(p.astype(BF16), vb, preferred_element_type=F32)
        outs.append(o / l)
    o_ref[...] = jnp.where(first, outs[0], outs[1]).astype(o_ref.dtype)


def _na(proj, bias):
    S = proj.shape[0]
    rows = S // GRID_W
    nq = NA_BLOCK_ROWS * GRID_W
    nk = NA_WIN_BLOCK_ROWS * GRID_W
    nb = S // nq
    qcol, kcol, vcol = QA_OFF // LANES, KA_OFF // LANES, VA_OFF // LANES

    def bias_map(hp, b):
        return (jnp.where(b == 0, 0, jnp.where(b == nb - 1, 2, 1)), hp, 0, 0)

    return pl.pallas_call(
        functools.partial(_na_kernel, rows=rows),
        grid=(NA_HEADS // 2, nb),
        in_specs=[
            pl.BlockSpec((nq, LANES), lambda hp, b: (b, qcol + hp)),
            pl.BlockSpec((S, LANES), lambda hp, b: (0, kcol + hp)),
            pl.BlockSpec((S, LANES), lambda hp, b: (0, vcol + hp)),
            pl.BlockSpec((1, 2, nq, nk), bias_map),
        ],
        out_specs=pl.BlockSpec((nq, LANES), lambda hp, b: (b, hp)),
        out_shape=jax.ShapeDtypeStruct((S, NA_HEADS * HEAD_DIM), BF16),
        compiler_params=_params("parallel", "arbitrary"),
        name="na_attn",
    )(proj, proj, proj, bias)


def _na_bias_tables(rpb, rows):
    br, wr = NA_BLOCK_ROWS, NA_WIN_BLOCK_ROWS
    c = np.arange(GRID_W)
    cs = np.clip(c - NA_WIN_COLS // 2, 0, GRID_W - NA_WIN_COLS)
    col_ok = (c[None, :] >= cs[:, None]) & (c[None, :] < cs[:, None] + NA_WIN_COLS)
    dc = np.clip(c[None, :] - c[:, None] + (NA_WIN_COLS - 1), 0, 2 * NA_WIN_COLS - 2)
    blocks = (0, 1, rows // br - 1)
    dr = np.zeros((3, br, wr), np.int32)
    ok = np.zeros((3, br, wr), bool)
    for t, b in enumerate(blocks):
        win0 = int(np.clip(b * br - NA_WIN_ROWS // 2, 0, rows - wr))
        for j in range(br):
            r = b * br + j
            rs = int(np.clip(r - NA_WIN_ROWS // 2, 0, rows - NA_WIN_ROWS))
            for w in range(wr):
                row = win0 + w
                if rs <= row < rs + NA_WIN_ROWS:
                    ok[t, j, w] = True
                    dr[t, j, w] = row - r + (NA_WIN_ROWS - 1)
    t_all = rpb.astype(F32)[:, :, :, dc]
    t_all = jnp.where(col_ok[None, None, None], t_all, NEG_INF)
    big = t_all[:, :, dr]
    big = jnp.where(ok[None, None, :, :, :, None, None], big, NEG_INF)
    big = jnp.transpose(big, (0, 2, 1, 3, 5, 4, 6))
    L, H = rpb.shape[0], rpb.shape[1]
    return big.reshape(L, 3, H, br * GRID_W, wr * GRID_W)


def _t5_bucket(rel):
    half = T5_BUCKETS // 2
    max_exact = half // 2
    ret = (rel > 0).astype(jnp.int32) * half
    n = jnp.abs(rel)
    nf = jnp.maximum(n, 1).astype(jnp.float32)
    large = max_exact + (jnp.log(nf / max_exact) / math.log(T5_MAX_DIST / max_exact)
                         * (half - max_exact)).astype(jnp.int32)
    large = jnp.minimum(large, half - 1)
    return ret + jnp.where(n < max_exact, n, large)


def _toeplitz(vals, n_rows, n_cols):
    length = n_rows + n_cols - 1
    lead = vals.shape[:-1]
    w = jnp.concatenate([vals, jnp.zeros(lead + (1,), vals.dtype)], axis=-1)
    flat = jnp.tile(w, (1,) * len(lead) + (n_rows,))[..., : n_rows * length]
    return flat.reshape(lead + (n_rows, length))[..., n_rows - 1:]


def _rel_bias(table, lo, hi):
    rel = jnp.arange(lo, hi + 1, dtype=jnp.int32)
    return jnp.transpose(table.astype(F32)[_t5_bucket(rel)], (1, 0))


def _diff_kernel(scal_ref, q_ref, k_ref, v_ref, b_ref, g_ref, lq_ref, o_ref,
                 qa_ref, qb_ref, s0_ref, s1_ref, t0_ref, t1_ref, p0_ref, p1_ref, a0_ref, a1_ref,
                 m_ref, acc_ref, *, nk):
    h = pl.program_id(0)
    qi = pl.program_id(1)
    tq, tk = DIFF_TQ, DIFF_TK
    two_d = 2 * HEAD_DIM

    q = q_ref[0].astype(F32)
    row = lax.broadcasted_iota(jnp.int32, (two_d, tq), 0)
    qa_ref[...] = jnp.where(row < HEAD_DIM, q, 0.0).astype(BF16)
    qb_ref[...] = jnp.where(row >= HEAD_DIM, q, 0.0).astype(BF16)
    m_ref[...] = jnp.full(m_ref.shape, NEG_INF, F32)
    acc_ref[...] = jnp.zeros(acc_ref.shape, F32)
    p1_ref[...] = jnp.zeros(p1_ref.shape, BF16)
    a1_ref[...] = jnp.ones(a1_ref.shape, F32)
    c_left = scal_ref[h]
    c_right = scal_ref[DIFF_HEADS + h]

    chunks = [(c, slice(j, j + DIFF_CHUNK)) for c in range(2) for j in range(0, tq, DIFF_CHUNK)]
    qm_refs = (qa_ref, qb_ref)

    def scores(kb, s_ref, t_ref, c, cols):
        s = jnp.dot(kb, qm_refs[c][:, cols], preferred_element_type=F32)
        s_ref[c, :, cols] = s
        t_ref[c, :, cols] = jnp.max(s, axis=0, keepdims=True)

    def add_bias(ki, s_ref, t_ref):
        d = ki - qi

        @pl.when(jnp.abs(d) <= 1)
        def _():
            bias = b_ref[0, d + 1]
            for c in range(2):
                s = s_ref[c] + bias
                s_ref[c] = s
                t_ref[c] = jnp.max(s, axis=0, keepdims=True)

    def softmax(shift, s_ref, t_ref, p_ref, a_ref, c, cols):
        m_old = m_ref[c, :, cols]
        m_new = jnp.maximum(m_old, t_ref[c, :, cols] + shift)
        a_ref[c, :, cols] = jnp.exp2(m_old - m_new)
        p_ref[c, :, cols] = jnp.exp2(s_ref[c, :, cols] - (m_new - shift)).astype(BF16)
        m_ref[c, :, cols] = m_new

    def values(vt, p_ref, a_ref, c, cols):
        acc_ref[c, :, cols] = (a_ref[c, :, cols] * acc_ref[c, :, cols]
                               + jnp.dot(vt, p_ref[c, :, cols], preferred_element_type=F32))

    def half(k_next, k_cur, k_prev, s_nxt, t_nxt, s_cur, t_cur, p_cur, a_cur, p_prv, a_prv):
        d = k_cur - qi
        shift = jnp.where(d < -1, c_left, jnp.where(d > 1, c_right, 0.0))
        kb = k_ref[pl.ds(pl.multiple_of(k_next * tk, tk), tk), :]
        vt = v_ref[0, k_prev]
        for c, cols in chunks:
            scores(kb, s_nxt, t_nxt, c, cols)
            softmax(shift, s_cur, t_cur, p_cur, a_cur, c, cols)
            values(vt, p_prv, a_prv, c, cols)
        add_bias(k_next, s_nxt, t_nxt)

    kb0 = k_ref[pl.ds(0, tk), :]
    for c, cols in chunks:
        scores(kb0, s0_ref, t0_ref, c, cols)
    add_bias(0, s0_ref, t0_ref)

    def pair(j, carry):
        k0 = 2 * j
        half(k0 + 1, k0, jnp.maximum(k0 - 1, 0), s1_ref, t1_ref, s0_ref, t0_ref, p0_ref, a0_ref, p1_ref, a1_ref)
        k2 = jnp.minimum(k0 + 2, nk - 1)
        half(k2, k0 + 1, k0, s0_ref, t0_ref, s1_ref, t1_ref, p1_ref, a1_ref, p0_ref, a0_ref)
        return carry

    lax.fori_loop(0, nk // 2, pair, 0)
    vt_last = v_ref[0, nk - 1]
    for c, cols in chunks:
        values(vt_last, p1_ref, a1_ref, c, cols)

    lam_init = scal_ref[2 * DIFF_HEADS]
    lq = lq_ref[...]
    lam = (jnp.exp(jnp.sum(lq[0:1] * lq[1:2], axis=-1, keepdims=True))
           - jnp.exp(jnp.sum(lq[2:3] * lq[3:4], axis=-1, keepdims=True)) + lam_init)
    acc0, acc1 = acc_ref[0], acc_ref[1]
    o = (acc0[:two_d] / acc0[two_d:two_d + 1]
         - lam * (acc1[:two_d] / acc1[two_d:two_d + 1]))
    y = o * lax.rsqrt(jnp.mean(o * o, axis=0, keepdims=True) + EPS) * g_ref[...]
    o_ref[0] = (y * (1.0 - lam_init)).astype(o_ref.dtype)


def _diff(scal, q_t, proj, v_t, bias_t, subln_g, lq):
    S = proj.shape[0]
    tq, tk = DIFF_TQ, DIFF_TK
    nk = S // tk
    assert nk % 2 == 0
    two_d = 2 * HEAD_DIM
    kcol = KD_OFF // LANES
    return pl.pallas_call(
        functools.partial(_diff_kernel, nk=nk),
        grid=(DIFF_HEADS, S // tq),
        in_specs=[
            pl.BlockSpec(memory_space=pltpu.SMEM),
            pl.BlockSpec((1, two_d, tq), lambda h, i: (h, 0, i)),
            pl.BlockSpec((S, LANES), lambda h, i: (0, kcol + h)),
            pl.BlockSpec((1, nk, DIFF_VROWS, tk), lambda h, i: (h, 0, 0, 0)),
            pl.BlockSpec((1, 3, tk, tq), lambda h, i: (h, 0, 0, 0)),
            _resident((two_d, 1), lambda h, i: (0, 0)),
            _resident((4, HEAD_DIM), lambda h, i: (0, 0)),
        ],
        out_specs=pl.BlockSpec((1, two_d, tq), lambda h, i: (h, 0, i)),
        out_shape=jax.ShapeDtypeStruct((DIFF_HEADS, two_d, S), BF16),
        scratch_shapes=[
            pltpu.VMEM((two_d, tq), BF16),
            pltpu.VMEM((two_d, tq), BF16),
            pltpu.VMEM((2, tk, tq), F32),
            pltpu.VMEM((2, tk, tq), F32),
            pltpu.VMEM((2, 1, tq), F32),
            pltpu.VMEM((2, 1, tq), F32),
            pltpu.VMEM((2, tk, tq), BF16),
            pltpu.VMEM((2, tk, tq), BF16),
            pltpu.VMEM((2, 1, tq), F32),
            pltpu.VMEM((2, 1, tq), F32),
            pltpu.VMEM((2, 1, tq), F32),
            pltpu.VMEM((2, DIFF_VROWS, tq), F32),
        ],
        compiler_params=_params("parallel", "arbitrary"),
        name="diff_attn",
    )(scal, q_t, proj, v_t, bias_t, subln_g, lq)


def _diff_bias_tiles(table):
    tq, tk = DIFF_TQ, DIFF_TK
    tiles = []
    for d in (-1, 0, 1):
        vals = _rel_bias(table, d * tk - (tq - 1), d * tk + tk - 1)
        tiles.append(jnp.swapaxes(_toeplitz(vals, tq, tk), -1, -2))
    return jnp.stack(tiles, axis=1)


def _gqa_kernel(q_ref, kp_ref, kc_ref, kn_ref, vp_ref, vc_ref, vn_ref, b_ref, sink_ref, o_ref, *, seq):
    blk = pl.program_id(1)
    qb, kw = GQA_QB, GQA_KW
    width = GQA_GROUP * HEAD_DIM
    q = q_ref[...].astype(F32)
    lane = lax.broadcasted_iota(jnp.int32, (qb, width), 1) // HEAD_DIM
    qs = jnp.concatenate(
        [jnp.where(lane == g, q, 0.0).astype(BF16) for g in range(GQA_GROUP)], axis=0)
    kx = jnp.concatenate([kp_ref[...], kc_ref[...], kn_ref[...]], axis=0)
    vx = jnp.concatenate([vp_ref[...], vc_ref[...], vn_ref[...]], axis=0)
    s = lax.dot_general(qs, kx, (((1,), (1,)), ((), ())), preferred_element_type=F32)
    s = s + b_ref[0]
    kpos = blk * qb - WINDOW + lax.broadcasted_iota(jnp.int32, (1, kw), 1)
    s = jnp.where((kpos >= 0) & (kpos < seq), s, NEG_INF)
    sink = sink_ref[0]
    m = jnp.maximum(jnp.max(s, axis=-1, keepdims=True), sink)
    p = jnp.exp(s - m)
    l = jnp.sum(p, axis=-1, keepdims=True) + jnp.exp(sink - m)
    o = jnp.dot(p.astype(BF16), vx, preferred_element_type=F32) / l
    out = jnp.zeros((qb, width), F32)
    for g in range(GQA_GROUP):
        out = jnp.where(lane == g, o[g * qb:(g + 1) * qb], out)
    o_ref[...] = out.astype(o_ref.dtype)


def _gqa(proj, kx, vx, bias, sink_rows):
    S = proj.shape[0]
    qb, kw = GQA_QB, GQA_KW
    width = GQA_GROUP * HEAD_DIM
    nb = S // qb
    sub = qb // WINDOW
    last = S // WINDOW - 1
    qcol = QC_OFF // width

    def prev_map(kv, b):
        return (jnp.maximum(b * sub - 1, 0), kv)

    def next_map(kv, b):
        return (jnp.minimum(b * sub + sub, last), kv)

    side = pl.BlockSpec((WINDOW, width), prev_map)
    nxt = pl.BlockSpec((WINDOW, width), next_map)
    cur = pl.BlockSpec((qb, width), lambda kv, b: (b, kv))
    return pl.pallas_call(
        functools.partial(_gqa_kernel, seq=S),
        grid=(GQA_KV_HEADS, nb),
        in_specs=[
            pl.BlockSpec((qb, width), lambda kv, b: (b, qcol + kv)),
            side, cur, nxt, side, cur, nxt,
            pl.BlockSpec((1, GQA_GROUP * qb, kw), lambda kv, b: (kv, 0, 0)),
            pl.BlockSpec((1, GQA_GROUP * qb, 1), lambda kv, b: (kv, 0, 0)),
        ],
        out_specs=pl.BlockSpec((qb, width), lambda kv, b: (b, kv)),
        out_shape=jax.ShapeDtypeStruct((S, GQA_Q_HEADS * HEAD_DIM), BF16),
        compiler_params=_params("parallel", "arbitrary"),
        name="gqa_attn",
    )(proj, kx, kx, kx, vx, vx, vx, bias, sink_rows)


def _gqa_bias(table):
    qb, kw = GQA_QB, GQA_KW
    vals = _rel_bias(table, -WINDOW - (qb - 1), kw - 1 - WINDOW)
    rel = jnp.arange(-WINDOW - (qb - 1), kw - WINDOW, dtype=jnp.int32)
    vals = jnp.where(jnp.abs(rel)[None, :] <= WINDOW, vals, NEG_INF)
    bias = _toeplitz(vals, qb, kw)
    return bias.reshape(GQA_KV_HEADS, GQA_GROUP * qb, kw)


def _merge_kernel(x_ref, ya_ref, yb_ref, yc_ref, g_ref, wg_ref, bg_ref, wb_ref, wo_ref, o_ref):
    x = x_ref[...]
    h = _rms(x, g_ref[...]).astype(BF16)
    gates = _sigmoid(jnp.dot(h, wg_ref[...], preferred_element_type=F32) + bg_ref[...])
    merged = None
    for i, y_ref in enumerate((ya_ref, yb_ref, yc_ref)):
        t = gates[:, i * D_MODEL:(i + 1) * D_MODEL] * jnp.dot(y_ref[...], wb_ref[i], preferred_element_type=F32)
        merged = t if merged is None else merged + t
    o_ref[...] = x + jnp.dot(merged.astype(BF16), wo_ref[...], preferred_element_type=F32)


def _merge(x, ya, yb, yc, g, wg, bg, wb, wo):
    S = x.shape[0]
    tm = ROW_TILE
    branch = pl.BlockSpec((tm, BRANCH_WIDTH), lambda i: (i, 0))
    return pl.pallas_call(
        _merge_kernel,
        grid=(S // tm,),
        in_specs=[
            pl.BlockSpec((tm, D_MODEL), lambda i: (i, 0)),
            branch, branch, branch,
            _resident((1, D_MODEL), lambda i: (0, 0)),
            _resident((D_MODEL, N_BRANCHES * D_MODEL), lambda i: (0, 0)),
            _resident((1, N_BRANCHES * D_MODEL), lambda i: (0, 0)),
            _resident((N_BRANCHES, BRANCH_WIDTH, D_MODEL), lambda i: (0, 0, 0)),
            _resident((D_MODEL, D_MODEL), lambda i: (0, 0)),
        ],
        out_specs=pl.BlockSpec((tm, D_MODEL), lambda i: (i, 0)),
        out_shape=jax.ShapeDtypeStruct((S, D_MODEL), F32),
        compiler_params=_params("parallel"),
        name="merge",
    )(x, ya, yb, yc, g, wg, bg, wb, wo)


def kernel(x, w_in, w_branch, w_gate, b_gate, w_o, norm_g, final_g, ffn_w_gate, ffn_w_up, ffn_w_down,
           na_rpb, diff_lambda, diff_subln_g, gqa_sink, rel_bias_table):
    B, S, _ = x.shape
    assert B == 1 and S % (NA_WIN_BLOCK_ROWS * GRID_W) == 0 and S // (NA_BLOCK_ROWS * GRID_W) >= 3
    depth = w_in.shape[0]
    two_d = 2 * HEAD_DIM
    nk = S // DIFF_TK

    col_scale = np.ones((W_IN_COLS,), np.float32)
    for off, width in ((QA_OFF, 512), (QD_OFF, 512), (QC_OFF, 512)):
        col_scale[off:off + width] = HEAD_DIM ** -0.5
    col_scale[QD_OFF:QD_OFF + 512] *= LOG2E
    w_in_b = (w_in * col_scale).astype(BF16)
    w_branch_b = w_branch.astype(BF16)
    w_gate_b = w_gate.astype(BF16)
    w_o_b = w_o.astype(BF16)
    ffn_wg_b = ffn_w_gate.astype(BF16)
    ffn_wu_b = ffn_w_up.astype(BF16)
    ffn_wd_b = ffn_w_down.astype(BF16)

    na_bias = _na_bias_tables(na_rpb, S // GRID_W)
    diff_table = rel_bias_table[:, :DIFF_HEADS]
    diff_bias = _diff_bias_tiles(diff_table) * LOG2E
    far = _rel_bias(diff_table, -T5_MAX_DIST, T5_MAX_DIST) * LOG2E
    ones_rows = jnp.zeros((DIFF_HEADS, nk, DIFF_VROWS - two_d, DIFF_TK), BF16).at[:, :, 0].set(1.0)
    gqa_bias = _gqa_bias(rel_bias_table[:, DIFF_HEADS:])
    final_row = final_g.reshape(1, D_MODEL).astype(F32)

    xs = x.reshape(S, D_MODEL)
    for l in range(depth):
        xs = _ffn(xs, norm_g[l, 0].reshape(1, D_MODEL), ffn_wg_b[l, 0], ffn_wu_b[l, 0], ffn_wd_b[l, 0],
                  final_row, False)
        g1 = norm_g[l, 1].reshape(1, D_MODEL)
        proj = _proj(xs, g1, w_in_b[l])

        ya = _na(proj, na_bias[l])

        lam_init = 0.8 - 0.6 * math.exp(-0.3 * l)
        scal = jnp.concatenate([far[:, 0], far[:, -1], jnp.full((1,), lam_init, F32)])
        q_t = proj[:, QD_OFF:QD_OFF + 512].T.reshape(DIFF_HEADS, two_d, S)
        v_t = proj[:, VD_OFF:VD_OFF + 512].T.reshape(DIFF_HEADS, two_d, nk, DIFF_TK).transpose(0, 2, 1, 3)
        v_t = jnp.concatenate([v_t, ones_rows], axis=2)
        yb_t = _diff(scal, q_t, proj, v_t, diff_bias, diff_subln_g[l].reshape(two_d, 1).astype(F32),
                     diff_lambda[l].astype(F32))
        yb = yb_t.reshape(DIFF_HEADS * two_d, S).T

        kx = jnp.tile(proj[:, KC_OFF:KC_OFF + 128].reshape(S, GQA_KV_HEADS, 1, HEAD_DIM),
                      (1, 1, GQA_GROUP, 1)).reshape(S, GQA_Q_HEADS * HEAD_DIM)
        vx = jnp.tile(proj[:, VC_OFF:VC_OFF + 128].reshape(S, GQA_KV_HEADS, 1, HEAD_DIM),
                      (1, 1, GQA_GROUP, 1)).reshape(S, GQA_Q_HEADS * HEAD_DIM)
        sink_rows = jnp.repeat(gqa_sink[l].astype(F32), GQA_QB).reshape(GQA_KV_HEADS, GQA_GROUP * GQA_QB, 1)
        yc = _gqa(proj, kx, vx, gqa_bias, sink_rows)

        xs = _merge(xs, ya, yb, yc, g1, w_gate_b[l], b_gate[l].reshape(1, -1).astype(F32),
                    w_branch_b[l], w_o_b[l])
        xs = _ffn(xs, norm_g[l, 2].reshape(1, D_MODEL), ffn_wg_b[l, 1], ffn_wu_b[l, 1], ffn_wd_b[l, 1],
                  final_row, l == depth - 1)
    return xs.reshape(B, S, D_MODEL)
```

```python
import functools
import math

import jax
import jax.numpy as jnp
import numpy as np
from jax import lax
from jax.experimental import pallas as pl
from jax.experimental.pallas import tpu as pltpu

D_MODEL = 1024
DEPTH = 4
HEAD_DIM = 64
NA_HEADS = 8
DIFF_HEADS = 4
GQA_Q_HEADS = 8
GQA_KV_HEADS = 2
GQA_GROUP = GQA_Q_HEADS // GQA_KV_HEADS
BRANCH_WIDTH = 512
N_BRANCHES = 3
D_FF = 2816
GRID_W = 64
NA_WIN_ROWS = 8
NA_WIN_COLS = 16
WINDOW = 128
T5_BUCKETS = 32
T5_MAX_DIST = 128
NEG_INF = -1e30
EPS = 1e-6

QA_OFF, KA_OFF, VA_OFF = 0, 512, 1024
QD_OFF, KD_OFF, VD_OFF = 1536, 2048, 2560
QC_OFF, KC_OFF, VC_OFF = 3072, 3584, 3712
W_IN_COLS = 3840

LANES = 128
VMEM_LIMIT = 56 * 1024 * 1024

ROW_TILE = 512
FFN_ROW_TILE = 256
NA_BLOCK_ROWS = 4
NA_WIN_BLOCK_ROWS = 12
DIFF_TQ = 512
DIFF_TK = 512
DIFF_CHUNK = 256
DIFF_VROWS = 2 * HEAD_DIM + 16
LOG2E = math.log2(math.e)
GQA_QB = 512

F32 = jnp.float32
BF16 = jnp.bfloat16


def _resident(shape, index_map):
    return pl.BlockSpec(shape, index_map, pipeline_mode=pl.Buffered(1))


def _params(*sem):
    return pltpu.CompilerParams(dimension_semantics=sem, vmem_limit_bytes=VMEM_LIMIT)


def _rms(x, g):
    return x * lax.rsqrt(jnp.mean(x * x, axis=-1, keepdims=True) + EPS) * g


def _sigmoid(z):
    return 1.0 / (1.0 + jnp.exp(-z))


def _ffn_kernel(x_ref, g_ref, wg_ref, wu_ref, wd_ref, fg_ref, o_ref, *, final):
    x = x_ref[...]
    h = _rms(x, g_ref[...]).astype(BF16)
    gate = jnp.dot(h, wg_ref[...], preferred_element_type=F32)
    up = jnp.dot(h, wu_ref[...], preferred_element_type=F32)
    a = (gate * _sigmoid(gate) * up).astype(BF16)
    y = x + 0.5 * jnp.dot(a, wd_ref[...], preferred_element_type=F32)
    if final:
        y = _rms(y, fg_ref[...])
    o_ref[...] = y


def _ffn(x, g, wg, wu, wd, fg, final):
    S = x.shape[0]
    tm = FFN_ROW_TILE
    return pl.pallas_call(
        functools.partial(_ffn_kernel, final=final),
        grid=(S // tm,),
        in_specs=[
            pl.BlockSpec((tm, D_MODEL), lambda i: (i, 0)),
            _resident((1, D_MODEL), lambda i: (0, 0)),
            _resident((D_MODEL, D_FF), lambda i: (0, 0)),
            _resident((D_MODEL, D_FF), lambda i: (0, 0)),
            _resident((D_FF, D_MODEL), lambda i: (0, 0)),
            _resident((1, D_MODEL), lambda i: (0, 0)),
        ],
        out_specs=pl.BlockSpec((tm, D_MODEL), lambda i: (i, 0)),
        out_shape=jax.ShapeDtypeStruct((S, D_MODEL), F32),
        compiler_params=_params("parallel"),
        name="ffn",
    )(x, g, wg, wu, wd, fg)


def _proj_kernel(x_ref, g_ref, w_ref, o_ref):
    h = _rms(x_ref[...], g_ref[...]).astype(BF16)
    o_ref[...] = jnp.dot(h, w_ref[...], preferred_element_type=F32).astype(BF16)


def _proj(x, g, w):
    S = x.shape[0]
    tm = ROW_TILE
    return pl.pallas_call(
        _proj_kernel,
        grid=(S // tm,),
        in_specs=[
            pl.BlockSpec((tm, D_MODEL), lambda i: (i, 0)),
            _resident((1, D_MODEL), lambda i: (0, 0)),
            _resident((D_MODEL, W_IN_COLS), lambda i: (0, 0)),
        ],
        out_specs=pl.BlockSpec((tm, W_IN_COLS), lambda i: (i, 0)),
        out_shape=jax.ShapeDtypeStruct((S, W_IN_COLS), BF16),
        compiler_params=_params("parallel"),
        name="proj",
    )(x, g, w)


def _staggered(n, stage_a, stage_b):
    cur = stage_a(0)
    for i in range(n):
        nxt = stage_a(i + 1) if i + 1 < n else None
        stage_b(i, cur)
        cur = nxt


def _na_kernel(q_ref, k0_ref, k1_ref, k2_ref, v0_ref, v1_ref, v2_ref, b_ref, o_ref):
    nq = NA_BLOCK_ROWS * GRID_W
    q = q_ref[...].astype(F32)
    k3 = jnp.concatenate([k0_ref[...], k1_ref[...], k2_ref[...]], axis=0)
    v3 = jnp.concatenate([v0_ref[...], v1_ref[...], v2_ref[...]], axis=0)
    first = lax.broadcasted_iota(jnp.int32, (nq, LANES), 1) < HEAD_DIM
    outs = [None] * NA_HEADS

    def logits(head):
        cols = slice((head // 2) * LANES, (head // 2 + 1) * LANES)
        keep = first if head % 2 == 0 else jnp.logical_not(first)
        qm = jnp.where(keep, q[:, cols], 0.0).astype(BF16)
        s = lax.dot_general(qm, k3[:, cols], (((1,), (1,)), ((), ())), preferred_element_type=F32)
        return s + b_ref[0, head]

    def attend(head, s):
        cols = slice((head // 2) * LANES, (head // 2 + 1) * LANES)
        m = jnp.max(s, axis=-1, keepdims=True)
        p = jnp.exp(s - m)
        l = jnp.sum(p, axis=-1, keepdims=True)
        outs[head] = jnp.dot(p.astype(BF16), v3[:, cols], preferred_element_type=F32) / l

    _staggered(NA_HEADS, logits, attend)
    for pair in range(NA_HEADS // 2):
        o_ref[:, pair * LANES:(pair + 1) * LANES] = jnp.where(
            first, outs[2 * pair], outs[2 * pair + 1]).astype(o_ref.dtype)


def _na(proj, bias):
    S = proj.shape[0]
    nq = NA_BLOCK_ROWS * GRID_W
    nb = S // nq
    width = NA_HEADS * HEAD_DIM
    qcol, kcol, vcol = QA_OFF // width, KA_OFF // width, VA_OFF // width

    def window(col, j):
        return pl.BlockSpec((nq, width), lambda b: (jnp.clip(b - 1, 0, nb - 3) + j, col))

    return pl.pallas_call(
        _na_kernel,
        grid=(nb,),
        in_specs=[
            pl.BlockSpec((nq, width), lambda b: (b, qcol)),
            window(kcol, 0), window(kcol, 1), window(kcol, 2),
            window(vcol, 0), window(vcol, 1), window(vcol, 2),
            pl.BlockSpec((1, NA_HEADS, nq, 3 * nq),
                         lambda b: (jnp.where(b == 0, 0, jnp.where(b == nb - 1, 2, 1)), 0, 0, 0)),
        ],
        out_specs=pl.BlockSpec((nq, width), lambda b: (b, 0)),
        out_shape=jax.ShapeDtypeStruct((S, width), BF16),
        compiler_params=_params("parallel"),
        name="na_attn",
    )(proj, proj, proj, proj, proj, proj, proj, bias)


def _na_bias_tables(rpb, rows):
    br, wr = NA_BLOCK_ROWS, NA_WIN_BLOCK_ROWS
    half = NA_WIN_ROWS // 2
    c = np.arange(GRID_W)
    cs = np.clip(c - NA_WIN_COLS // 2, 0, GRID_W - NA_WIN_COLS)
    col_ok = (c[None, :] >= cs[:, None]) & (c[None, :] < cs[:, None] + NA_WIN_COLS)
    pad = GRID_W - NA_WIN_COLS
    vals = jnp.pad(rpb.astype(F32), ((0, 0), (0, 0), (0, 0), (pad, pad)))
    t_all = jnp.where(col_ok, _toeplitz(vals, GRID_W, GRID_W), NEG_INF)
    L, H = rpb.shape[0], rpb.shape[1]
    types = []
    for b in (0, 1, rows // br - 1):
        win0 = int(np.clip(b * br - half, 0, rows - wr))
        row_blocks = []
        for j in range(br):
            r = b * br + j
            rs = int(np.clip(r - half, 0, rows - NA_WIN_ROWS))
            w_lo, dr_lo = rs - win0, rs - r + (NA_WIN_ROWS - 1)
            piece = jnp.moveaxis(t_all[:, :, dr_lo:dr_lo + NA_WIN_ROWS], 2, 3)
            piece = piece.reshape(L, H, GRID_W, NA_WIN_ROWS * GRID_W)
            row_blocks.append(jnp.pad(
                piece, ((0, 0), (0, 0), (0, 0), (w_lo * GRID_W, (wr - NA_WIN_ROWS - w_lo) * GRID_W)),
                constant_values=NEG_INF))
        types.append(jnp.stack(row_blocks, axis=2).reshape(L, H, br * GRID_W, wr * GRID_W))
    return jnp.stack(types, axis=1)


def _t5_bucket(rel):
    half = T5_BUCKETS // 2
    max_exact = half // 2
    ret = (rel > 0).astype(jnp.int32) * half
    n = jnp.abs(rel)
    nf = jnp.maximum(n, 1).astype(jnp.float32)
    large = max_exact + (jnp.log(nf / max_exact) / math.log(T5_MAX_DIST / max_exact)
                         * (half - max_exact)).astype(jnp.int32)
    large = jnp.minimum(large, half - 1)
    return ret + jnp.where(n < max_exact, n, large)


def _toeplitz(vals, n_rows, n_cols):
    length = n_rows + n_cols - 1
    lead = vals.shape[:-1]
    w = jnp.concatenate([vals, jnp.zeros(lead + (1,), vals.dtype)], axis=-1)
    flat = jnp.tile(w, (1,) * len(lead) + (n_rows,))[..., : n_rows * length]
    return flat.reshape(lead + (n_rows, length))[..., n_rows - 1:]


def _rel_bias(table, lo, hi):
    rel = jnp.arange(lo, hi + 1, dtype=jnp.int32)
    return jnp.transpose(table.astype(F32)[_t5_bucket(rel)], (1, 0))


def _diff_kernel(scal_ref, q_ref, k_ref, v_ref, b_ref, g_ref, lq_ref, o_ref,
                 qa_ref, qb_ref, s0_ref, s1_ref, t0_ref, t1_ref, p0_ref, p1_ref, a0_ref, a1_ref,
                 m_ref, acc_ref, *, nk):
    h = pl.program_id(0)
    qi = pl.program_id(1)
    tq, tk = DIFF_TQ, DIFF_TK
    two_d = 2 * HEAD_DIM

    q = q_ref[0].astype(F32)
    row = lax.broadcasted_iota(jnp.int32, (two_d, tq), 0)
    qa_ref[...] = jnp.where(row < HEAD_DIM, q, 0.0).astype(BF16)
    qb_ref[...] = jnp.where(row >= HEAD_DIM, q, 0.0).astype(BF16)
    m_ref[...] = jnp.full(m_ref.shape, NEG_INF, F32)
    acc_ref[...] = jnp.zeros(acc_ref.shape, F32)
    p1_ref[...] = jnp.zeros(p1_ref.shape, BF16)
    a1_ref[...] = jnp.ones(a1_ref.shape, F32)
    c_left = scal_ref[h]
    c_right = scal_ref[DIFF_HEADS + h]

    chunks = [(c, slice(j, j + DIFF_CHUNK)) for c in range(2) for j in range(0, tq, DIFF_CHUNK)]
    qm_refs = (qa_ref, qb_ref)

    def scores(kb, s_ref, t_ref, c, cols):
        s = jnp.dot(kb, qm_refs[c][:, cols], preferred_element_type=F32)
        s_ref[c, :, cols] = s
        t_ref[c, :, cols] = jnp.max(s, axis=0, keepdims=True)

    def add_bias(ki, s_ref, t_ref):
        d = ki - qi

        @pl.when(jnp.abs(d) <= 1)
        def _():
            bias = b_ref[0, d + 1]
            for c in range(2):
                s = s_ref[c] + bias
                s_ref[c] = s
                t_ref[c] = jnp.max(s, axis=0, keepdims=True)

    def softmax(shift, s_ref, t_ref, p_ref, a_ref, c, cols):
        m_old = m_ref[c, :, cols]
        m_new = jnp.maximum(m_old, t_ref[c, :, cols] + shift)
        a_ref[c, :, cols] = jnp.exp2(m_old - m_new)
        p_ref[c, :, cols] = jnp.exp2(s_ref[c, :, cols] - (m_new - shift)).astype(BF16)
        m_ref[c, :, cols] = m_new

    def values(vt, p_ref, a_ref, c, cols):
        acc_ref[c, :, cols] = (a_ref[c, :, cols] * acc_ref[c, :, cols]
                               + jnp.dot(vt, p_ref[c, :, cols], preferred_element_type=F32))

    def half(k_next, k_cur, k_prev, s_nxt, t_nxt, s_cur, t_cur, p_cur, a_cur, p_prv, a_prv):
        d = k_cur - qi
        shift = jnp.where(d < -1, c_left, jnp.where(d > 1, c_right, 0.0))
        kb = k_ref[pl.ds(pl.multiple_of(k_next * tk, tk), tk), :]
        vt = v_ref[0, k_prev]
        for c, cols in chunks:
            scores(kb, s_nxt, t_nxt, c, cols)
            softmax(shift, s_cur, t_cur, p_cur, a_cur, c, cols)
            values(vt, p_prv, a_prv, c, cols)
        add_bias(k_next, s_nxt, t_nxt)

    kb0 = k_ref[pl.ds(0, tk), :]
    for c, cols in chunks:
        scores(kb0, s0_ref, t0_ref, c, cols)
    add_bias(0, s0_ref, t0_ref)

    def pair(j, carry):
        k0 = 2 * j
        half(k0 + 1, k0, jnp.maximum(k0 - 1, 0), s1_ref, t1_ref, s0_ref, t0_ref, p0_ref, a0_ref, p1_ref, a1_ref)
        k2 = jnp.minimum(k0 + 2, nk - 1)
        half(k2, k0 + 1, k0, s0_ref, t0_ref, s1_ref, t1_ref, p1_ref, a1_ref, p0_ref, a0_ref)
        return carry

    lax.fori_loop(0, nk // 2, pair, 0)
    vt_last = v_ref[0, nk - 1]
    for c, cols in chunks:
        values(vt_last, p1_ref, a1_ref, c, cols)

    lam_init = scal_ref[2 * DIFF_HEADS]
    lq = lq_ref[...]
    lam = (jnp.exp(jnp.sum(lq[0:1] * lq[1:2], axis=-1, keepdims=True))
           - jnp.exp(jnp.sum(lq[2:3] * lq[3:4], axis=-1, keepdims=True)) + lam_init)
    acc0, acc1 = acc_ref[0], acc_ref[1]
    o = (acc0[:two_d] / acc0[two_d:two_d + 1]
         - lam * (acc1[:two_d] / acc1[two_d:two_d + 1]))
    y = o * lax.rsqrt(jnp.mean(o * o, axis=0, keepdims=True) + EPS) * g_ref[...]
    o_ref[0] = (y * (1.0 - lam_init)).astype(o_ref.dtype)


def _diff(scal, q_t, proj, v_t, bias_t, subln_g, lq):
    S = proj.shape[0]
    tq, tk = DIFF_TQ, DIFF_TK
    nk = S // tk
    assert nk % 2 == 0
    two_d = 2 * HEAD_DIM
    kcol = KD_OFF // LANES
    return pl.pallas_call(
        functools.partial(_diff_kernel, nk=nk),
        grid=(DIFF_HEADS, S // tq),
        in_specs=[
            pl.BlockSpec(memory_space=pltpu.SMEM),
            pl.BlockSpec((1, two_d, tq), lambda h, i: (h, 0, i)),
            pl.BlockSpec((S, LANES), lambda h, i: (0, kcol + h)),
            pl.BlockSpec((1, nk, DIFF_VROWS, tk), lambda h, i: (h, 0, 0, 0)),
            pl.BlockSpec((1, 3, tk, tq), lambda h, i: (h, 0, 0, 0)),
            _resident((two_d, 1), lambda h, i: (0, 0)),
            _resident((4, HEAD_DIM), lambda h, i: (0, 0)),
        ],
        out_specs=pl.BlockSpec((1, two_d, tq), lambda h, i: (h, 0, i)),
        out_shape=jax.ShapeDtypeStruct((DIFF_HEADS, two_d, S), BF16),
        scratch_shapes=[
            pltpu.VMEM((two_d, tq), BF16),
            pltpu.VMEM((two_d, tq), BF16),
            pltpu.VMEM((2, tk, tq), F32),
            pltpu.VMEM((2, tk, tq), F32),
            pltpu.VMEM((2, 1, tq), F32),
            pltpu.VMEM((2, 1, tq), F32),
            pltpu.VMEM((2, tk, tq), BF16),
            pltpu.VMEM((2, tk, tq), BF16),
            pltpu.VMEM((2, 1, tq), F32),
            pltpu.VMEM((2, 1, tq), F32),
            pltpu.VMEM((2, 1, tq), F32),
            pltpu.VMEM((2, DIFF_VROWS, tq), F32),
        ],
        compiler_params=_params("parallel", "arbitrary"),
        name="diff_attn",
    )(scal, q_t, proj, v_t, bias_t, subln_g, lq)


def _diff_bias_tiles(table):
    tq, tk = DIFF_TQ, DIFF_TK
    tiles = []
    for d in (-1, 0, 1):
        vals = _rel_bias(table, d * tk - (tq - 1), d * tk + tk - 1)
        tiles.append(jnp.swapaxes(_toeplitz(vals, tq, tk), -1, -2))
    return jnp.stack(tiles, axis=1)


def _gqa_kernel(q_ref, kp_ref, kc_ref, kn_ref, vp_ref, vc_ref, vn_ref, b_ref, sink_ref, o_ref, *, n_blocks):
    blk = pl.program_id(1)
    sub = GQA_QB // WINDOW
    kw = 3 * WINDOW
    width = GQA_GROUP * HEAD_DIM
    q = q_ref[...].astype(F32)
    kx = jnp.concatenate([kp_ref[...], kc_ref[...], kn_ref[...]], axis=0)
    vx = jnp.concatenate([vp_ref[...], vc_ref[...], vn_ref[...]], axis=0)
    lane = lax.broadcasted_iota(jnp.int32, (WINDOW, width), 1) // HEAD_DIM
    kcol = lax.broadcasted_iota(jnp.int32, (1, kw), 1)
    sink = sink_ref[0]

    def logits(i):
        qi = q[i * WINDOW:(i + 1) * WINDOW]
        qs = jnp.concatenate([jnp.where(lane == g, qi, 0.0).astype(BF16) for g in range(GQA_GROUP)], axis=0)
        s = lax.dot_general(qs, kx[i * WINDOW:i * WINDOW + kw], (((1,), (1,)), ((), ())),
                            preferred_element_type=F32) + b_ref[0]
        if i == 0:
            s = jnp.where((kcol < WINDOW) & (blk == 0), NEG_INF, s)
        if i == sub - 1:
            s = jnp.where((kcol >= 2 * WINDOW) & (blk == n_blocks - 1), NEG_INF, s)
        return s

    def attend(i, s):
        m = jnp.maximum(jnp.max(s, axis=-1, keepdims=True), sink)
        p = jnp.exp(s - m)
        l = jnp.sum(p, axis=-1, keepdims=True) + jnp.exp(sink - m)
        o = jnp.dot(p.astype(BF16), vx[i * WINDOW:i * WINDOW + kw], preferred_element_type=F32) / l
        out = o[:WINDOW]
        for g in range(1, GQA_GROUP):
            out = jnp.where(lane == g, o[g * WINDOW:(g + 1) * WINDOW], out)
        o_ref[i * WINDOW:(i + 1) * WINDOW, :] = out.astype(o_ref.dtype)

    _staggered(sub, logits, attend)


def _gqa(proj, kx, vx, bias, sink_rows):
    S = proj.shape[0]
    qb, kw = GQA_QB, 3 * WINDOW
    width = GQA_GROUP * HEAD_DIM
    nb = S // qb
    sub = qb // WINDOW
    last = S // WINDOW - 1
    qcol = QC_OFF // width

    def prev_map(kv, b):
        return (jnp.maximum(b * sub - 1, 0), kv)

    def next_map(kv, b):
        return (jnp.minimum(b * sub + sub, last), kv)

    side = pl.BlockSpec((WINDOW, width), prev_map)
    nxt = pl.BlockSpec((WINDOW, width), next_map)
    cur = pl.BlockSpec((qb, width), lambda kv, b: (b, kv))
    return pl.pallas_call(
        functools.partial(_gqa_kernel, n_blocks=nb),
        grid=(GQA_KV_HEADS, nb),
        in_specs=[
            pl.BlockSpec((qb, width), lambda kv, b: (b, qcol + kv)),
            side, cur, nxt, side, cur, nxt,
            pl.BlockSpec((1, GQA_GROUP * WINDOW, kw), lambda kv, b: (kv, 0, 0)),
            pl.BlockSpec((1, GQA_GROUP * WINDOW, 1), lambda kv, b: (kv, 0, 0)),
        ],
        out_specs=pl.BlockSpec((qb, width), lambda kv, b: (b, kv)),
        out_shape=jax.ShapeDtypeStruct((S, GQA_Q_HEADS * HEAD_DIM), BF16),
        compiler_params=_params("parallel", "arbitrary"),
        name="gqa_attn",
    )(proj, kx, kx, kx, vx, vx, vx, bias, sink_rows)


def _gqa_bias(table):
    qb, kw = WINDOW, 3 * WINDOW
    vals = _rel_bias(table, -WINDOW - (qb - 1), kw - 1 - WINDOW)
    rel = jnp.arange(-WINDOW - (qb - 1), kw - WINDOW, dtype=jnp.int32)
    vals = jnp.where(jnp.abs(rel)[None, :] <= WINDOW, vals, NEG_INF)
    bias = _toeplitz(vals, qb, kw)
    return bias.reshape(GQA_KV_HEADS, GQA_GROUP * qb, kw)


def _merge_kernel(x_ref, ya_ref, yb_ref, yc_ref, g_ref, wg_ref, bg_ref, wb_ref, wo_ref, o_ref):
    x = x_ref[...]
    h = _rms(x, g_ref[...]).astype(BF16)
    gates = _sigmoid(jnp.dot(h, wg_ref[...], preferred_element_type=F32) + bg_ref[...])
    merged = None
    for i, y_ref in enumerate((ya_ref, yb_ref, yc_ref)):
        t = gates[:, i * D_MODEL:(i + 1) * D_MODEL] * jnp.dot(y_ref[...], wb_ref[i], preferred_element_type=F32)
        merged = t if merged is None else merged + t
    o_ref[...] = x + jnp.dot(merged.astype(BF16), wo_ref[...], preferred_element_type=F32)


def _merge(x, ya, yb, yc, g, wg, bg, wb, wo):
    S = x.shape[0]
    tm = ROW_TILE
    branch = pl.BlockSpec((tm, BRANCH_WIDTH), lambda i: (i, 0))
    return pl.pallas_call(
        _merge_kernel,
        grid=(S // tm,),
        in_specs=[
            pl.BlockSpec((tm, D_MODEL), lambda i: (i, 0)),
            branch, branch, branch,
            _resident((1, D_MODEL), lambda i: (0, 0)),
            _resident((D_MODEL, N_BRANCHES * D_MODEL), lambda i: (0, 0)),
            _resident((1, N_BRANCHES * D_MODEL), lambda i: (0, 0)),
            _resident((N_BRANCHES, BRANCH_WIDTH, D_MODEL), lambda i: (0, 0, 0)),
            _resident((D_MODEL, D_MODEL), lambda i: (0, 0)),
        ],
        out_specs=pl.BlockSpec((tm, D_MODEL), lambda i: (i, 0)),
        out_shape=jax.ShapeDtypeStruct((S, D_MODEL), F32),
        compiler_params=_params("parallel"),
        name="merge",
    )(x, ya, yb, yc, g, wg, bg, wb, wo)


def kernel(x, w_in, w_branch, w_gate, b_gate, w_o, norm_g, final_g, ffn_w_gate, ffn_w_up, ffn_w_down,
           na_rpb, diff_lambda, diff_subln_g, gqa_sink, rel_bias_table):
    B, S, _ = x.shape
    assert B == 1 and S % (2 * DIFF_TK) == 0 and S // (NA_BLOCK_ROWS * GRID_W) >= 4
    depth = w_in.shape[0]
    two_d = 2 * HEAD_DIM
    nk = S // DIFF_TK

    col_scale = np.ones((W_IN_COLS,), np.float32)
    for off, width in ((QA_OFF, 512), (QD_OFF, 512), (QC_OFF, 512)):
        col_scale[off:off + width] = HEAD_DIM ** -0.5
    col_scale[QD_OFF:QD_OFF + 512] *= LOG2E
    w_in_b = (w_in * col_scale).astype(BF16)
    w_branch_b = w_branch.astype(BF16)
    w_gate_b = w_gate.astype(BF16)
    w_o_b = w_o.astype(BF16)
    ffn_wg_b = ffn_w_gate.astype(BF16)
    ffn_wu_b = ffn_w_up.astype(BF16)
    ffn_wd_b = ffn_w_down.astype(BF16)

    na_bias = _na_bias_tables(na_rpb, S // GRID_W)
    diff_table = rel_bias_table[:, :DIFF_HEADS]
    diff_bias = _diff_bias_tiles(diff_table) * LOG2E
    far = _rel_bias(diff_table, -T5_MAX_DIST, T5_MAX_DIST) * LOG2E
    ones_rows = jnp.zeros((DIFF_HEADS, nk, DIFF_VROWS - two_d, DIFF_TK), BF16).at[:, :, 0].set(1.0)
    gqa_bias = _gqa_bias(rel_bias_table[:, DIFF_HEADS:])
    final_row = final_g.reshape(1, D_MODEL).astype(F32)

    xs = x.reshape(S, D_MODEL)
    for l in range(depth):
        xs = _ffn(xs, norm_g[l, 0].reshape(1, D_MODEL), ffn_wg_b[l, 0], ffn_wu_b[l, 0], ffn_wd_b[l, 0],
                  final_row, False)
        g1 = norm_g[l, 1].reshape(1, D_MODEL)
        proj = _proj(xs, g1, w_in_b[l])

        ya = _na(proj, na_bias[l])

        lam_init = 0.8 - 0.6 * math.exp(-0.3 * l)
        scal = jnp.concatenate([far[:, 0], far[:, -1], jnp.full((1,), lam_init, F32)])
        q_t = proj[:, QD_OFF:QD_OFF + 512].T.reshape(DIFF_HEADS, two_d, S)
        v_t = proj[:, VD_OFF:VD_OFF + 512].T.reshape(DIFF_HEADS, two_d, nk, DIFF_TK).transpose(0, 2, 1, 3)
        v_t = jnp.concatenate([v_t, ones_rows], axis=2)
        yb_t = _diff(scal, q_t, proj, v_t, diff_bias, diff_subln_g[l].reshape(two_d, 1).astype(F32),
                     diff_lambda[l].astype(F32))
        yb = yb_t.reshape(DIFF_HEADS * two_d, S).T

        kx = jnp.tile(proj[:, KC_OFF:KC_OFF + 128].reshape(S, GQA_KV_HEADS, 1, HEAD_DIM),
                      (1, 1, GQA_GROUP, 1)).reshape(S, GQA_Q_HEADS * HEAD_DIM)
        vx = jnp.tile(proj[:, VC_OFF:VC_OFF + 128].reshape(S, GQA_KV_HEADS, 1, HEAD_DIM),
                      (1, 1, GQA_GROUP, 1)).reshape(S, GQA_Q_HEADS * HEAD_DIM)
        sink_rows = jnp.repeat(gqa_sink[l].astype(F32), WINDOW).reshape(GQA_KV_HEADS, GQA_GROUP * WINDOW, 1)
        yc = _gqa(proj, kx, vx, gqa_bias, sink_rows)

        xs = _merge(xs, ya, yb, yc, g1, w_gate_b[l], b_gate[l].reshape(1, -1).astype(F32),
                    w_branch_b[l], w_o_b[l])
        xs = _ffn(xs, norm_g[l, 2].reshape(1, D_MODEL), ffn_wg_b[l, 1], ffn_wu_b[l, 1], ffn_wd_b[l, 1],
                  final_row, l == depth - 1)
    return xs.reshape(B, S, D_MODEL)
```

```python
import functools
import math

import jax
import jax.numpy as jnp
import numpy as np
from jax import lax
from jax.experimental import pallas as pl
from jax.experimental.pallas import tpu as pltpu

D_MODEL = 1024
DEPTH = 4
HEAD_DIM = 64
NA_HEADS = 8
DIFF_HEADS = 4
GQA_Q_HEADS = 8
GQA_KV_HEADS = 2
GQA_GROUP = GQA_Q_HEADS // GQA_KV_HEADS
BRANCH_WIDTH = 512
N_BRANCHES = 3
D_FF = 2816
GRID_W = 64
NA_WIN_ROWS = 8
NA_WIN_COLS = 16
WINDOW = 128
T5_BUCKETS = 32
T5_MAX_DIST = 128
NEG_INF = -1e30
EPS = 1e-6

QA_OFF, KA_OFF, VA_OFF = 0, 512, 1024
QD_OFF, KD_OFF, VD_OFF = 1536, 2048, 2560
QC_OFF, KC_OFF, VC_OFF = 3072, 3584, 3712
W_IN_COLS = 3840

LANES = 128
VMEM_LIMIT = 56 * 1024 * 1024

ROW_TILE = 512
FFN_ROW_TILE = 256
NA_BLOCK_ROWS = 4
NA_WIN_BLOCK_ROWS = 12
DIFF_TQ = 512
DIFF_TK = 512
DIFF_CHUNK = 256
DIFF_PAIRS_PER_TRIP = 3
DIFF_VROWS = 2 * HEAD_DIM + 16
LOG2E = math.log2(math.e)
GQA_QB = 512

F32 = jnp.float32
BF16 = jnp.bfloat16


def _resident(shape, index_map):
    return pl.BlockSpec(shape, index_map, pipeline_mode=pl.Buffered(1))


def _params(*sem, flags=None):
    return pltpu.CompilerParams(dimension_semantics=sem, vmem_limit_bytes=VMEM_LIMIT, flags=flags)


def _rms(x, g):
    return x * lax.rsqrt(jnp.mean(x * x, axis=-1, keepdims=True) + EPS) * g


def _sigmoid(z):
    return 1.0 / (1.0 + jnp.exp(-z))


def _ffn_kernel(x_ref, g_ref, wg_ref, wu_ref, wd_ref, fg_ref, o_ref, *, final):
    x = x_ref[...]
    h = _rms(x, g_ref[...]).astype(BF16)
    gate = jnp.dot(h, wg_ref[...], preferred_element_type=F32)
    up = jnp.dot(h, wu_ref[...], preferred_element_type=F32)
    a = (gate * _sigmoid(gate) * up).astype(BF16)
    y = x + 0.5 * jnp.dot(a, wd_ref[...], preferred_element_type=F32)
    if final:
        y = _rms(y, fg_ref[...])
    o_ref[...] = y


def _ffn(x, g, wg, wu, wd, fg, final):
    S = x.shape[0]
    tm = FFN_ROW_TILE
    return pl.pallas_call(
        functools.partial(_ffn_kernel, final=final),
        grid=(S // tm,),
        in_specs=[
            pl.BlockSpec((tm, D_MODEL), lambda i: (i, 0)),
            _resident((1, D_MODEL), lambda i: (0, 0)),
            _resident((D_MODEL, D_FF), lambda i: (0, 0)),
            _resident((D_MODEL, D_FF), lambda i: (0, 0)),
            _resident((D_FF, D_MODEL), lambda i: (0, 0)),
            _resident((1, D_MODEL), lambda i: (0, 0)),
        ],
        out_specs=pl.BlockSpec((tm, D_MODEL), lambda i: (i, 0)),
        out_shape=jax.ShapeDtypeStruct((S, D_MODEL), F32),
        compiler_params=_params("parallel"),
        name="ffn",
    )(x, g, wg, wu, wd, fg)


def _proj_kernel(x_ref, g_ref, w_ref, o_ref):
    h = _rms(x_ref[...], g_ref[...]).astype(BF16)
    o_ref[...] = jnp.dot(h, w_ref[...], preferred_element_type=F32).astype(BF16)


def _proj(x, g, w):
    S = x.shape[0]
    tm = ROW_TILE
    return pl.pallas_call(
        _proj_kernel,
        grid=(S // tm,),
        in_specs=[
            pl.BlockSpec((tm, D_MODEL), lambda i: (i, 0)),
            _resident((1, D_MODEL), lambda i: (0, 0)),
            _resident((D_MODEL, W_IN_COLS), lambda i: (0, 0)),
        ],
        out_specs=pl.BlockSpec((tm, W_IN_COLS), lambda i: (i, 0)),
        out_shape=jax.ShapeDtypeStruct((S, W_IN_COLS), BF16),
        compiler_params=_params("parallel"),
        name="proj",
    )(x, g, w)


def _staggered(n, stage_a, stage_b):
    cur = stage_a(0)
    for i in range(n):
        nxt = stage_a(i + 1) if i + 1 < n else None
        stage_b(i, cur)
        cur = nxt


def _na_kernel(q_ref, k0_ref, k1_ref, k2_ref, v0_ref, v1_ref, v2_ref, b_ref, o_ref):
    nq = NA_BLOCK_ROWS * GRID_W
    q = q_ref[...].astype(F32)
    k3 = jnp.concatenate([k0_ref[...], k1_ref[...], k2_ref[...]], axis=0)
    v3 = jnp.concatenate([v0_ref[...], v1_ref[...], v2_ref[...]], axis=0)
    first = lax.broadcasted_iota(jnp.int32, (nq, LANES), 1) < HEAD_DIM
    outs = [None] * NA_HEADS

    def logits(head):
        cols = slice((head // 2) * LANES, (head // 2 + 1) * LANES)
        keep = first if head % 2 == 0 else jnp.logical_not(first)
        qm = jnp.where(keep, q[:, cols], 0.0).astype(BF16)
        s = lax.dot_general(qm, k3[:, cols], (((1,), (1,)), ((), ())), preferred_element_type=F32)
        return s + b_ref[0, head]

    def attend(head, s):
        cols = slice((head // 2) * LANES, (head // 2 + 1) * LANES)
        m = jnp.max(s, axis=-1, keepdims=True)
        p = jnp.exp(s - m)
        l = jnp.sum(p, axis=-1, keepdims=True)
        outs[head] = jnp.dot(p.astype(BF16), v3[:, cols], preferred_element_type=F32) / l

    _staggered(NA_HEADS, logits, attend)
    for pair in range(NA_HEADS // 2):
        o_ref[:, pair * LANES:(pair + 1) * LANES] = jnp.where(
            first, outs[2 * pair], outs[2 * pair + 1]).astype(o_ref.dtype)


def _na(proj, bias):
    S = proj.shape[0]
    nq = NA_BLOCK_ROWS * GRID_W
    nb = S // nq
    width = NA_HEADS * HEAD_DIM
    qcol, kcol, vcol = QA_OFF // width, KA_OFF // width, VA_OFF // width

    def window(col, j):
        return pl.BlockSpec((nq, width), lambda b: (jnp.clip(b - 1, 0, nb - 3) + j, col))

    return pl.pallas_call(
        _na_kernel,
        grid=(nb,),
        in_specs=[
            pl.BlockSpec((nq, width), lambda b: (b, qcol)),
            window(kcol, 0), window(kcol, 1), window(kcol, 2),
            window(vcol, 0), window(vcol, 1), window(vcol, 2),
            pl.BlockSpec((1, NA_HEADS, nq, 3 * nq),
                         lambda b: (jnp.where(b == 0, 0, jnp.where(b == nb - 1, 2, 1)), 0, 0, 0)),
        ],
        out_specs=pl.BlockSpec((nq, width), lambda b: (b, 0)),
        out_shape=jax.ShapeDtypeStruct((S, width), BF16),
        compiler_params=_params("parallel"),
        name="na_attn",
    )(proj, proj, proj, proj, proj, proj, proj, bias)


def _na_bias_tables(rpb, rows):
    br, wr = NA_BLOCK_ROWS, NA_WIN_BLOCK_ROWS
    half = NA_WIN_ROWS // 2
    c = np.arange(GRID_W)
    cs = np.clip(c - NA_WIN_COLS // 2, 0, GRID_W - NA_WIN_COLS)
    col_ok = (c[None, :] >= cs[:, None]) & (c[None, :] < cs[:, None] + NA_WIN_COLS)
    pad = GRID_W - NA_WIN_COLS
    vals = jnp.pad(rpb.astype(F32), ((0, 0), (0, 0), (0, 0), (pad, pad)))
    t_all = jnp.where(col_ok, _toeplitz(vals, GRID_W, GRID_W), NEG_INF)
    L, H = rpb.shape[0], rpb.shape[1]
    types = []
    for b in (0, 1, rows // br - 1):
        win0 = int(np.clip(b * br - half, 0, rows - wr))
        row_blocks = []
        for j in range(br):
            r = b * br + j
            rs = int(np.clip(r - half, 0, rows - NA_WIN_ROWS))
            w_lo, dr_lo = rs - win0, rs - r + (NA_WIN_ROWS - 1)
            piece = jnp.moveaxis(t_all[:, :, dr_lo:dr_lo + NA_WIN_ROWS], 2, 3)
            piece = piece.reshape(L, H, GRID_W, NA_WIN_ROWS * GRID_W)
            row_blocks.append(jnp.pad(
                piece, ((0, 0), (0, 0), (0, 0), (w_lo * GRID_W, (wr - NA_WIN_ROWS - w_lo) * GRID_W)),
                constant_values=NEG_INF))
        types.append(jnp.stack(row_blocks, axis=2).reshape(L, H, br * GRID_W, wr * GRID_W))
    return jnp.stack(types, axis=1)


def _t5_bucket(rel):
    half = T5_BUCKETS // 2
    max_exact = half // 2
    ret = (rel > 0).astype(jnp.int32) * half
    n = jnp.abs(rel)
    nf = jnp.maximum(n, 1).astype(jnp.float32)
    large = max_exact + (jnp.log(nf / max_exact) / math.log(T5_MAX_DIST / max_exact)
                         * (half - max_exact)).astype(jnp.int32)
    large = jnp.minimum(large, half - 1)
    return ret + jnp.where(n < max_exact, n, large)


def _toeplitz(vals, n_rows, n_cols):
    length = n_rows + n_cols - 1
    lead = vals.shape[:-1]
    w = jnp.concatenate([vals, jnp.zeros(lead + (1,), vals.dtype)], axis=-1)
    flat = jnp.tile(w, (1,) * len(lead) + (n_rows,))[..., : n_rows * length]
    return flat.reshape(lead + (n_rows, length))[..., n_rows - 1:]


def _rel_bias(table, lo, hi):
    rel = jnp.arange(lo, hi + 1, dtype=jnp.int32)
    return jnp.transpose(table.astype(F32)[_t5_bucket(rel)], (1, 0))


def _diff_kernel(scal_ref, q_ref, k_ref, v_ref, b_ref, g_ref, lq_ref, o_ref,
                 qa_ref, qb_ref, s0_ref, s1_ref, t0_ref, t1_ref, p0_ref, p1_ref, a0_ref, a1_ref,
                 m_ref, acc_ref, *, nk):
    h = pl.program_id(0)
    qi = pl.program_id(1)
    tq, tk = DIFF_TQ, DIFF_TK
    two_d = 2 * HEAD_DIM

    q = q_ref[0].astype(F32)
    row = lax.broadcasted_iota(jnp.int32, (two_d, tq), 0)
    qa_ref[...] = jnp.where(row < HEAD_DIM, q, 0.0).astype(BF16)
    qb_ref[...] = jnp.where(row >= HEAD_DIM, q, 0.0).astype(BF16)
    m_ref[...] = jnp.full(m_ref.shape, NEG_INF, F32)
    acc_ref[...] = jnp.zeros(acc_ref.shape, F32)
    c_left = scal_ref[h]
    c_right = scal_ref[DIFF_HEADS + h]

    chunks = [(c, slice(j, j + DIFF_CHUNK)) for c in range(2) for j in range(0, tq, DIFF_CHUNK)]
    qm_refs = (qa_ref, qb_ref)
    even = (s0_ref, t0_ref, p0_ref, a0_ref)
    odd = (s1_ref, t1_ref, p1_ref, a1_ref)

    def scores(kb, bias_fn, bufs, c, cols):
        s_ref, t_ref = bufs[0], bufs[1]
        s = jnp.dot(kb, qm_refs[c][:, cols], preferred_element_type=F32)
        if bias_fn is not None:
            s = s + bias_fn(cols)
        s_ref[c, :, cols] = s
        t_ref[c, :, cols] = jnp.max(s, axis=0, keepdims=True)

    def softmax(shift, bufs, c, cols):
        s_ref, t_ref, p_ref, a_ref = bufs
        m_old = m_ref[c, :, cols]
        m_new = jnp.maximum(m_old, t_ref[c, :, cols] + shift)
        a_ref[c, :, cols] = jnp.exp2(m_old - m_new)
        p_ref[c, :, cols] = jnp.exp2(s_ref[c, :, cols] - (m_new - shift)).astype(BF16)
        m_ref[c, :, cols] = m_new

    def values(vt, bufs, c, cols):
        p_ref, a_ref = bufs[2], bufs[3]
        acc_ref[c, :, cols] = (a_ref[c, :, cols] * acc_ref[c, :, cols]
                               + jnp.dot(vt, p_ref[c, :, cols], preferred_element_type=F32))

    def block(nxt, cur, prv):
        for c, cols in chunks:
            if nxt is not None:
                kb = k_ref[pl.ds(pl.multiple_of(nxt[0] * tk, tk), tk), :]
                scores(kb, nxt[1], nxt[2], c, cols)
            if cur is not None:
                softmax(cur[0], cur[1], c, cols)
            if prv is not None:
                vt = v_ref[0, prv[0]]
                values(vt, prv[1], c, cols)

    k_left = jnp.maximum(qi - 1, 0)
    k_right = jnp.minimum(qi + 1, nk - 1)
    has_left = qi >= 1
    has_right = qi + 1 < nk

    def near_bias(index, exists):
        def fn(cols):
            bias = b_ref[0, index, :, cols]
            return bias if exists is None else jnp.where(exists, bias, NEG_INF)
        return fn

    n_near = k_right - k_left + 1
    n_slots = nk - 2

    def far(f):
        ki = jnp.minimum(jnp.where(f < k_left, f, f + n_near), nk - 1)
        shift = jnp.where(f < nk - n_near, jnp.where(ki < qi, c_left, c_right), NEG_INF)
        return ki, shift

    far0, shift0 = far(0)
    block((qi, near_bias(1, None), even), None, None)
    block((k_left, near_bias(0, has_left), odd), (0.0, even), None)
    block((k_right, near_bias(2, has_right), even), (0.0, odd), (qi, even))
    block((far0, None, odd), (0.0, even), (k_left, odd))

    def pair(f, k_prev):
        k_a, shift_a = far(f)
        k_b, shift_b = far(f + 1)
        k_c, _ = far(f + 2)
        block((k_b, None, even), (shift_a, odd), (k_prev, even))
        block((k_c, None, odd), (shift_b, even), (k_a, odd))
        return k_b

    def pairs(j, k_prev):
        for i in range(DIFF_PAIRS_PER_TRIP):
            k_prev = pair(2 * (DIFF_PAIRS_PER_TRIP * j + i), k_prev)
        return k_prev

    assert n_slots % (2 * DIFF_PAIRS_PER_TRIP) == 0
    k_last = lax.fori_loop(0, n_slots // (2 * DIFF_PAIRS_PER_TRIP), pairs, k_right)
    block(None, None, (k_last, even))

    lam_init = scal_ref[2 * DIFF_HEADS]
    lq = lq_ref[...]
    lam = (jnp.exp(jnp.sum(lq[0:1] * lq[1:2], axis=-1, keepdims=True))
           - jnp.exp(jnp.sum(lq[2:3] * lq[3:4], axis=-1, keepdims=True)) + lam_init)
    acc0, acc1 = acc_ref[0], acc_ref[1]
    o = (acc0[:two_d] / acc0[two_d:two_d + 1]
         - lam * (acc1[:two_d] / acc1[two_d:two_d + 1]))
    y = o * lax.rsqrt(jnp.mean(o * o, axis=0, keepdims=True) + EPS) * g_ref[...]
    o_ref[0] = (y * (1.0 - lam_init)).astype(o_ref.dtype)


def _diff(scal, q_t, proj, v_t, bias_t, subln_g, lq):
    S = proj.shape[0]
    tq, tk = DIFF_TQ, DIFF_TK
    nk = S // tk
    assert nk % 2 == 0
    two_d = 2 * HEAD_DIM
    kcol = KD_OFF // LANES
    return pl.pallas_call(
        functools.partial(_diff_kernel, nk=nk),
        grid=(DIFF_HEADS, S // tq),
        in_specs=[
            pl.BlockSpec(memory_space=pltpu.SMEM),
            pl.BlockSpec((1, two_d, tq), lambda h, i: (h, 0, i)),
            pl.BlockSpec((S, LANES), lambda h, i: (0, kcol + h)),
            pl.BlockSpec((1, nk, DIFF_VROWS, tk), lambda h, i: (h, 0, 0, 0)),
            pl.BlockSpec((1, 3, tk, tq), lambda h, i: (h, 0, 0, 0)),
            _resident((two_d, 1), lambda h, i: (0, 0)),
            _resident((4, HEAD_DIM), lambda h, i: (0, 0)),
        ],
        out_specs=pl.BlockSpec((1, two_d, tq), lambda h, i: (h, 0, i)),
        out_shape=jax.ShapeDtypeStruct((DIFF_HEADS, two_d, S), BF16),
        scratch_shapes=[
            pltpu.VMEM((two_d, tq), BF16),
            pltpu.VMEM((two_d, tq), BF16),
            pltpu.VMEM((2, tk, tq), F32),
            pltpu.VMEM((2, tk, tq), F32),
            pltpu.VMEM((2, 1, tq), F32),
            pltpu.VMEM((2, 1, tq), F32),
            pltpu.VMEM((2, tk, tq), BF16),
            pltpu.VMEM((2, tk, tq), BF16),
            pltpu.VMEM((2, 1, tq), F32),
            pltpu.VMEM((2, 1, tq), F32),
            pltpu.VMEM((2, 1, tq), F32),
            pltpu.VMEM((2, DIFF_VROWS, tq), F32),
        ],
        compiler_params=_params("parallel", "arbitrary"),
        name="diff_attn",
    )(scal, q_t, proj, v_t, bias_t, subln_g, lq)


def _diff_bias_tiles(table):
    tq, tk = DIFF_TQ, DIFF_TK
    tiles = []
    for d in (-1, 0, 1):
        vals = _rel_bias(table, d * tk - (tq - 1), d * tk + tk - 1)
        tiles.append(jnp.swapaxes(_toeplitz(vals, tq, tk), -1, -2))
    return jnp.stack(tiles, axis=1)


def _gqa_kernel(q_ref, kp_ref, kc_ref, kn_ref, vp_ref, vc_ref, vn_ref, b_ref, sink_ref, o_ref, *, n_blocks):
    blk = pl.program_id(1)
    sub = GQA_QB // WINDOW
    kw = 3 * WINDOW
    width = GQA_GROUP * HEAD_DIM
    q = q_ref[...].astype(F32)
    kx = jnp.concatenate([kp_ref[...], kc_ref[...], kn_ref[...]], axis=0)
    vx = jnp.concatenate([vp_ref[...], vc_ref[...], vn_ref[...]], axis=0)
    lane = lax.broadcasted_iota(jnp.int32, (WINDOW, width), 1) // HEAD_DIM
    kcol = lax.broadcasted_iota(jnp.int32, (1, kw), 1)
    sink = sink_ref[0]

    def logits(i):
        qi = q[i * WINDOW:(i + 1) * WINDOW]
        qs = jnp.concatenate([jnp.where(lane == g, qi, 0.0).astype(BF16) for g in range(GQA_GROUP)], axis=0)
        s = lax.dot_general(qs, kx[i * WINDOW:i * WINDOW + kw], (((1,), (1,)), ((), ())),
                            preferred_element_type=F32) + b_ref[0]
        if i == 0:
            s = jnp.where((kcol < WINDOW) & (blk == 0), NEG_INF, s)
        if i == sub - 1:
            s = jnp.where((kcol >= 2 * WINDOW) & (blk == n_blocks - 1), NEG_INF, s)
        return s

    def attend(i, s):
        m = jnp.maximum(jnp.max(s, axis=-1, keepdims=True), sink)
        p = jnp.exp(s - m)
        l = jnp.sum(p, axis=-1, keepdims=True) + jnp.exp(sink - m)
        o = jnp.dot(p.astype(BF16), vx[i * WINDOW:i * WINDOW + kw], preferred_element_type=F32) / l
        out = o[:WINDOW]
        for g in range(1, GQA_GROUP):
            out = jnp.where(lane == g, o[g * WINDOW:(g + 1) * WINDOW], out)
        o_ref[i * WINDOW:(i + 1) * WINDOW, :] = out.astype(o_ref.dtype)

    _staggered(sub, logits, attend)


def _gqa(proj, kx, vx, bias, sink_rows):
    S = proj.shape[0]
    qb, kw = GQA_QB, 3 * WINDOW
    width = GQA_GROUP * HEAD_DIM
    nb = S // qb
    sub = qb // WINDOW
    last = S // WINDOW - 1
    qcol = QC_OFF // width

    def prev_map(kv, b):
        return (jnp.maximum(b * sub - 1, 0), kv)

    def next_map(kv, b):
        return (jnp.minimum(b * sub + sub, last), kv)

    side = pl.BlockSpec((WINDOW, width), prev_map)
    nxt = pl.BlockSpec((WINDOW, width), next_map)
    cur = pl.BlockSpec((qb, width), lambda kv, b: (b, kv))
    return pl.pallas_call(
        functools.partial(_gqa_kernel, n_blocks=nb),
        grid=(GQA_KV_HEADS, nb),
        in_specs=[
            pl.BlockSpec((qb, width), lambda kv, b: (b, qcol + kv)),
            side, cur, nxt, side, cur, nxt,
            pl.BlockSpec((1, GQA_GROUP * WINDOW, kw), lambda kv, b: (kv, 0, 0)),
            pl.BlockSpec((1, GQA_GROUP * WINDOW, 1), lambda kv, b: (kv, 0, 0)),
        ],
        out_specs=pl.BlockSpec((qb, width), lambda kv, b: (b, kv)),
        out_shape=jax.ShapeDtypeStruct((S, GQA_Q_HEADS * HEAD_DIM), BF16),
        compiler_params=_params("parallel", "arbitrary"),
        name="gqa_attn",
    )(proj, kx, kx, kx, vx, vx, vx, bias, sink_rows)


def _gqa_bias(table):
    qb, kw = WINDOW, 3 * WINDOW
    vals = _rel_bias(table, -WINDOW - (qb - 1), kw - 1 - WINDOW)
    rel = jnp.arange(-WINDOW - (qb - 1), kw - WINDOW, dtype=jnp.int32)
    vals = jnp.where(jnp.abs(rel)[None, :] <= WINDOW, vals, NEG_INF)
    bias = _toeplitz(vals, qb, kw)
    return bias.reshape(GQA_KV_HEADS, GQA_GROUP * qb, kw)


def _merge_kernel(x_ref, ya_ref, yb_ref, yc_ref, g_ref, wg_ref, bg_ref, wb_ref, wo_ref, o_ref):
    x = x_ref[...]
    h = _rms(x, g_ref[...]).astype(BF16)
    gates = _sigmoid(jnp.dot(h, wg_ref[...], preferred_element_type=F32) + bg_ref[...])
    merged = None
    for i, y_ref in enumerate((ya_ref, yb_ref, yc_ref)):
        t = gates[:, i * D_MODEL:(i + 1) * D_MODEL] * jnp.dot(y_ref[...], wb_ref[i], preferred_element_type=F32)
        merged = t if merged is None else merged + t
    o_ref[...] = x + jnp.dot(merged.astype(BF16), wo_ref[...], preferred_element_type=F32)


def _merge(x, ya, yb, yc, g, wg, bg, wb, wo):
    S = x.shape[0]
    tm = ROW_TILE
    branch = pl.BlockSpec((tm, BRANCH_WIDTH), lambda i: (i, 0))
    return pl.pallas_call(
        _merge_kernel,
        grid=(S // tm,),
        in_specs=[
            pl.BlockSpec((tm, D_MODEL), lambda i: (i, 0)),
            branch, branch, branch,
            _resident((1, D_MODEL), lambda i: (0, 0)),
            _resident((D_MODEL, N_BRANCHES * D_MODEL), lambda i: (0, 0)),
            _resident((1, N_BRANCHES * D_MODEL), lambda i: (0, 0)),
            _resident((N_BRANCHES, BRANCH_WIDTH, D_MODEL), lambda i: (0, 0, 0)),
            _resident((D_MODEL, D_MODEL), lambda i: (0, 0)),
        ],
        out_specs=pl.BlockSpec((tm, D_MODEL), lambda i: (i, 0)),
        out_shape=jax.ShapeDtypeStruct((S, D_MODEL), F32),
        compiler_params=_params("parallel"),
        name="merge",
    )(x, ya, yb, yc, g, wg, bg, wb, wo)


def kernel(x, w_in, w_branch, w_gate, b_gate, w_o, norm_g, final_g, ffn_w_gate, ffn_w_up, ffn_w_down,
           na_rpb, diff_lambda, diff_subln_g, gqa_sink, rel_bias_table):
    B, S, _ = x.shape
    assert B == 1 and S % (2 * DIFF_TK) == 0 and S // (NA_BLOCK_ROWS * GRID_W) >= 4
    depth = w_in.shape[0]
    two_d = 2 * HEAD_DIM
    nk = S // DIFF_TK

    col_scale = np.ones((W_IN_COLS,), np.float32)
    for off, width in ((QA_OFF, 512), (QD_OFF, 512), (QC_OFF, 512)):
        col_scale[off:off + width] = HEAD_DIM ** -0.5
    col_scale[QD_OFF:QD_OFF + 512] *= LOG2E
    w_in_b = (w_in * col_scale).astype(BF16)
    w_branch_b = w_branch.astype(BF16)
    w_gate_b = w_gate.astype(BF16)
    w_o_b = w_o.astype(BF16)
    ffn_wg_b = ffn_w_gate.astype(BF16)
    ffn_wu_b = ffn_w_up.astype(BF16)
    ffn_wd_b = ffn_w_down.astype(BF16)

    na_bias = _na_bias_tables(na_rpb, S // GRID_W)
    diff_table = rel_bias_table[:, :DIFF_HEADS]
    diff_bias = _diff_bias_tiles(diff_table) * LOG2E
    far = _rel_bias(diff_table, -T5_MAX_DIST, T5_MAX_DIST) * LOG2E
    ones_rows = jnp.zeros((DIFF_HEADS, nk, DIFF_VROWS - two_d, DIFF_TK), BF16).at[:, :, 0].set(1.0)
    gqa_bias = _gqa_bias(rel_bias_table[:, DIFF_HEADS:])
    final_row = final_g.reshape(1, D_MODEL).astype(F32)

    xs = x.reshape(S, D_MODEL)
    for l in range(depth):
        xs = _ffn(xs, norm_g[l, 0].reshape(1, D_MODEL), ffn_wg_b[l, 0], ffn_wu_b[l, 0], ffn_wd_b[l, 0],
                  final_row, False)
        g1 = norm_g[l, 1].reshape(1, D_MODEL)
        proj = _proj(xs, g1, w_in_b[l])

        ya = _na(proj, na_bias[l])

        lam_init = 0.8 - 0.6 * math.exp(-0.3 * l)
        scal = jnp.concatenate([far[:, 0], far[:, -1], jnp.full((1,), lam_init, F32)])
        q_t = proj[:, QD_OFF:QD_OFF + 512].T.reshape(DIFF_HEADS, two_d, S)
        v_t = proj[:, VD_OFF:VD_OFF + 512].T.reshape(DIFF_HEADS, two_d, nk, DIFF_TK).transpose(0, 2, 1, 3)
        v_t = jnp.concatenate([v_t, ones_rows], axis=2)
        yb_t = _diff(scal, q_t, proj, v_t, diff_bias, diff_subln_g[l].reshape(two_d, 1).astype(F32),
                     diff_lambda[l].astype(F32))
        yb = yb_t.reshape(DIFF_HEADS * two_d, S).T

        kx = jnp.tile(proj[:, KC_OFF:KC_OFF + 128].reshape(S, GQA_KV_HEADS, 1, HEAD_DIM),
                      (1, 1, GQA_GROUP, 1)).reshape(S, GQA_Q_HEADS * HEAD_DIM)
        vx = jnp.tile(proj[:, VC_OFF:VC_OFF + 128].reshape(S, GQA_KV_HEADS, 1, HEAD_DIM),
                      (1, 1, GQA_GROUP, 1)).reshape(S, GQA_Q_HEADS * HEAD_DIM)
        sink_rows = jnp.repeat(gqa_sink[l].astype(F32), WINDOW).reshape(GQA_KV_HEADS, GQA_GROUP * WINDOW, 1)
        yc = _gqa(proj, kx, vx, gqa_bias, sink_rows)

        xs = _merge(xs, ya, yb, yc, g1, w_gate_b[l], b_gate[l].reshape(1, -1).astype(F32),
                    w_branch_b[l], w_o_b[l])
        xs = _ffn(xs, norm_g[l, 2].reshape(1, D_MODEL), ffn_wg_b[l, 1], ffn_wu_b[l, 1], ffn_wd_b[l, 1],
                  final_row, l == depth - 1)
    return xs.reshape(B, S, D_MODEL)
```

```python
import functools
import math

import jax
import jax.numpy as jnp
import numpy as np
from jax import lax
from jax.experimental import pallas as pl
from jax.experimental.pallas import tpu as pltpu

D_MODEL = 1024
DEPTH = 4
HEAD_DIM = 64
NA_HEADS = 8
DIFF_HEADS = 4
GQA_Q_HEADS = 8
GQA_KV_HEADS = 2
GQA_GROUP = GQA_Q_HEADS // GQA_KV_HEADS
BRANCH_WIDTH = 512
N_BRANCHES = 3
D_FF = 2816
GRID_W = 64
NA_WIN_ROWS = 8
NA_WIN_COLS = 16
WINDOW = 128
T5_BUCKETS = 32
T5_MAX_DIST = 128
NEG_INF = -1e30
EPS = 1e-6

QA_OFF, KA_OFF, VA_OFF = 0, 512, 1024
QD_OFF, KD_OFF, VD_OFF = 1536, 2048, 2560
QC_OFF, KC_OFF, VC_OFF = 3072, 3584, 3712
W_IN_COLS = 3840
N_QA, N_KA, N_VA, N_KD, N_QC, N_KC, N_VC = 0, 512, 1024, 1536, 2048, 2560, 2688
PROJ_N_COLS = 2816
PROJ_T_ROWS = 1024

LANES = 128
VMEM_LIMIT = 56 * 1024 * 1024

ROW_TILE = 512
FFN_ROW_TILE = 256
NA_BLOCK_ROWS = 4
NA_WIN_BLOCK_ROWS = 12
DIFF_TQ = 512
DIFF_TK = 512
DIFF_CHUNK = 256
DIFF_PAIRS_PER_TRIP = 3
DIFF_VROWS = 2 * HEAD_DIM + 16
LOG2E = math.log2(math.e)
GQA_QB = 512

F32 = jnp.float32
BF16 = jnp.bfloat16


def _resident(shape, index_map):
    return pl.BlockSpec(shape, index_map, pipeline_mode=pl.Buffered(1))


def _params(*sem, flags=None):
    return pltpu.CompilerParams(dimension_semantics=sem, vmem_limit_bytes=VMEM_LIMIT, flags=flags)


def _rms(x, g):
    return x * lax.rsqrt(jnp.mean(x * x, axis=-1, keepdims=True) + EPS) * g


def _sigmoid(z):
    return 1.0 / (1.0 + jnp.exp(-z))


def _ffn_kernel(x_ref, g_ref, wg_ref, wu_ref, wd_ref, fg_ref, o_ref, *, final):
    x = x_ref[...]
    h = _rms(x, g_ref[...]).astype(BF16)
    gate = jnp.dot(h, wg_ref[...], preferred_element_type=F32)
    up = jnp.dot(h, wu_ref[...], preferred_element_type=F32)
    a = (gate * _sigmoid(gate) * up).astype(BF16)
    y = x + 0.5 * jnp.dot(a, wd_ref[...], preferred_element_type=F32)
    if final:
        y = _rms(y, fg_ref[...])
    o_ref[...] = y


def _ffn(x, g, wg, wu, wd, fg, layer, which, final):
    S = x.shape[0]
    tm = FFN_ROW_TILE
    pick = lambda i: (layer, which, 0, 0)
    return pl.pallas_call(
        functools.partial(_ffn_kernel, final=final),
        grid=(S // tm,),
        in_specs=[
            pl.BlockSpec((tm, D_MODEL), lambda i: (i, 0)),
            _resident((1, D_MODEL), lambda i: (0, 0)),
            _resident((None, None, D_MODEL, D_FF), pick),
            _resident((None, None, D_MODEL, D_FF), pick),
            _resident((None, None, D_FF, D_MODEL), pick),
            _resident((1, D_MODEL), lambda i: (0, 0)),
        ],
        out_specs=pl.BlockSpec((tm, D_MODEL), lambda i: (i, 0)),
        out_shape=jax.ShapeDtypeStruct((S, D_MODEL), F32),
        compiler_params=_params("parallel"),
        name="ffn",
    )(x, g, wg, wu, wd, fg)


def _proj_kernel(x_ref, g_ref, wn_ref, wt_ref, on_ref, ot_ref):
    h = _rms(x_ref[...], g_ref[...]).astype(BF16)
    on_ref[...] = jnp.dot(h, wn_ref[...], preferred_element_type=F32).astype(BF16)
    ot_ref[0] = lax.dot_general(wt_ref[...], h, (((1,), (1,)), ((), ())),
                                preferred_element_type=F32).astype(BF16)


def _proj(x, g, wn, wt, layer):
    S = x.shape[0]
    tm = ROW_TILE
    return pl.pallas_call(
        _proj_kernel,
        grid=(S // tm,),
        in_specs=[
            pl.BlockSpec((tm, D_MODEL), lambda i: (i, 0)),
            _resident((1, D_MODEL), lambda i: (0, 0)),
            _resident((None, D_MODEL, PROJ_N_COLS), lambda i: (layer, 0, 0)),
            _resident((None, PROJ_T_ROWS, D_MODEL), lambda i: (layer, 0, 0)),
        ],
        out_specs=[
            pl.BlockSpec((tm, PROJ_N_COLS), lambda i: (i, 0)),
            pl.BlockSpec((1, PROJ_T_ROWS, tm), lambda i: (i, 0, 0)),
        ],
        out_shape=[
            jax.ShapeDtypeStruct((S, PROJ_N_COLS), BF16),
            jax.ShapeDtypeStruct((S // tm, PROJ_T_ROWS, tm), BF16),
        ],
        compiler_params=_params("parallel"),
        name="proj",
    )(x, g, wn, wt)


def _staggered(n, stage_a, stage_b):
    cur = stage_a(0)
    for i in range(n):
        nxt = stage_a(i + 1) if i + 1 < n else None
        stage_b(i, cur)
        cur = nxt


def _na_kernel(q_ref, k0_ref, k1_ref, k2_ref, v0_ref, v1_ref, v2_ref, b_ref, o_ref):
    nq = NA_BLOCK_ROWS * GRID_W
    q = q_ref[...].astype(F32)
    k3 = jnp.concatenate([k0_ref[...], k1_ref[...], k2_ref[...]], axis=0)
    v3 = jnp.concatenate([v0_ref[...], v1_ref[...], v2_ref[...]], axis=0)
    first = lax.broadcasted_iota(jnp.int32, (nq, LANES), 1) < HEAD_DIM
    outs = [None] * NA_HEADS

    def logits(head):
        cols = slice((head // 2) * LANES, (head // 2 + 1) * LANES)
        keep = first if head % 2 == 0 else jnp.logical_not(first)
        qm = jnp.where(keep, q[:, cols], 0.0).astype(BF16)
        s = lax.dot_general(qm, k3[:, cols], (((1,), (1,)), ((), ())), preferred_element_type=F32)
        return s + b_ref[0, head]

    def attend(head, s):
        cols = slice((head // 2) * LANES, (head // 2 + 1) * LANES)
        m = jnp.max(s, axis=-1, keepdims=True)
        p = jnp.exp(s - m)
        l = jnp.sum(p, axis=-1, keepdims=True)
        outs[head] = jnp.dot(p.astype(BF16), v3[:, cols], preferred_element_type=F32) / l

    _staggered(NA_HEADS, logits, attend)
    for pair in range(NA_HEADS // 2):
        o_ref[:, pair * LANES:(pair + 1) * LANES] = jnp.where(
            first, outs[2 * pair], outs[2 * pair + 1]).astype(o_ref.dtype)


def _na(proj, bias):
    S = proj.shape[0]
    nq = NA_BLOCK_ROWS * GRID_W
    nb = S // nq
    width = NA_HEADS * HEAD_DIM
    qcol, kcol, vcol = N_QA // width, N_KA // width, N_VA // width

    def window(col, j):
        return pl.BlockSpec((nq, width), lambda b: (jnp.clip(b - 1, 0, nb - 3) + j, col))

    return pl.pallas_call(
        _na_kernel,
        grid=(nb,),
        in_specs=[
            pl.BlockSpec((nq, width), lambda b: (b, qcol)),
            window(kcol, 0), window(kcol, 1), window(kcol, 2),
            window(vcol, 0), window(vcol, 1), window(vcol, 2),
            pl.BlockSpec((1, NA_HEADS, nq, 3 * nq),
                         lambda b: (jnp.where(b == 0, 0, jnp.where(b == nb - 1, 2, 1)), 0, 0, 0)),
        ],
        out_specs=pl.BlockSpec((nq, width), lambda b: (b, 0)),
        out_shape=jax.ShapeDtypeStruct((S, width), BF16),
        compiler_params=_params("parallel"),
        name="na_attn",
    )(proj, proj, proj, proj, proj, proj, proj, bias)


def _na_bias_tables(rpb, rows):
    br, wr = NA_BLOCK_ROWS, NA_WIN_BLOCK_ROWS
    half = NA_WIN_ROWS // 2
    c = np.arange(GRID_W)
    cs = np.clip(c - NA_WIN_COLS // 2, 0, GRID_W - NA_WIN_COLS)
    col_ok = (c[None, :] >= cs[:, None]) & (c[None, :] < cs[:, None] + NA_WIN_COLS)
    pad = GRID_W - NA_WIN_COLS
    vals = jnp.pad(rpb.astype(F32), ((0, 0), (0, 0), (0, 0), (pad, pad)))
    t_all = jnp.where(col_ok, _toeplitz(vals, GRID_W, GRID_W), NEG_INF)
    L, H = rpb.shape[0], rpb.shape[1]
    types = []
    for b in (0, 1, rows // br - 1):
        win0 = int(np.clip(b * br - half, 0, rows - wr))
        row_blocks = []
        for j in range(br):
            r = b * br + j
            rs = int(np.clip(r - half, 0, rows - NA_WIN_ROWS))
            w_lo, dr_lo = rs - win0, rs - r + (NA_WIN_ROWS - 1)
            piece = jnp.moveaxis(t_all[:, :, dr_lo:dr_lo + NA_WIN_ROWS], 2, 3)
            piece = piece.reshape(L, H, GRID_W, NA_WIN_ROWS * GRID_W)
            row_blocks.append(jnp.pad(
                piece, ((0, 0), (0, 0), (0, 0), (w_lo * GRID_W, (wr - NA_WIN_ROWS - w_lo) * GRID_W)),
                constant_values=NEG_INF))
        types.append(jnp.stack(row_blocks, axis=2).reshape(L, H, br * GRID_W, wr * GRID_W))
    return jnp.stack(types, axis=1)


def _t5_bucket(rel):
    half = T5_BUCKETS // 2
    max_exact = half // 2
    ret = (rel > 0).astype(jnp.int32) * half
    n = jnp.abs(rel)
    nf = jnp.maximum(n, 1).astype(jnp.float32)
    large = max_exact + (jnp.log(nf / max_exact) / math.log(T5_MAX_DIST / max_exact)
                         * (half - max_exact)).astype(jnp.int32)
    large = jnp.minimum(large, half - 1)
    return ret + jnp.where(n < max_exact, n, large)


def _toeplitz(vals, n_rows, n_cols):
    length = n_rows + n_cols - 1
    lead = vals.shape[:-1]
    w = jnp.concatenate([vals, jnp.zeros(lead + (1,), vals.dtype)], axis=-1)
    flat = jnp.tile(w, (1,) * len(lead) + (n_rows,))[..., : n_rows * length]
    return flat.reshape(lead + (n_rows, length))[..., n_rows - 1:]


def _rel_bias(table, lo, hi):
    rel = jnp.arange(lo, hi + 1, dtype=jnp.int32)
    return jnp.transpose(table.astype(F32)[_t5_bucket(rel)], (1, 0))


def _diff_kernel(scal_ref, q_ref, k_ref, v_ref, b_ref, g_ref, lq_ref, o_ref,
                 qa_ref, qb_ref, s0_ref, s1_ref, t0_ref, t1_ref, p0_ref, p1_ref, a0_ref, a1_ref,
                 m_ref, acc_ref, *, nk):
    h = pl.program_id(0)
    qi = pl.program_id(1)
    tq, tk = DIFF_TQ, DIFF_TK
    two_d = 2 * HEAD_DIM

    q = q_ref[0].astype(F32)
    row = lax.broadcasted_iota(jnp.int32, (two_d, tq), 0)
    qa_ref[...] = jnp.where(row < HEAD_DIM, q, 0.0).astype(BF16)
    qb_ref[...] = jnp.where(row >= HEAD_DIM, q, 0.0).astype(BF16)
    m_ref[...] = jnp.full(m_ref.shape, NEG_INF, F32)
    acc_ref[...] = jnp.zeros(acc_ref.shape, F32)
    c_left = scal_ref[h]
    c_right = scal_ref[DIFF_HEADS + h]
    ones_rows = jnp.where(lax.broadcasted_iota(jnp.int32, (DIFF_VROWS - two_d, tk), 0) == 0,
                          1.0, 0.0).astype(BF16)

    chunks = [(c, slice(j, j + DIFF_CHUNK)) for c in range(2) for j in range(0, tq, DIFF_CHUNK)]
    qm_refs = (qa_ref, qb_ref)
    even = (s0_ref, t0_ref, p0_ref, a0_ref)
    odd = (s1_ref, t1_ref, p1_ref, a1_ref)

    def scores(kb, bias_fn, bufs, c, cols):
        s_ref, t_ref = bufs[0], bufs[1]
        s = jnp.dot(kb, qm_refs[c][:, cols], preferred_element_type=F32)
        if bias_fn is not None:
            s = s + bias_fn(cols)
        s_ref[c, :, cols] = s
        t_ref[c, :, cols] = jnp.max(s, axis=0, keepdims=True)

    def softmax(shift, bufs, c, cols):
        s_ref, t_ref, p_ref, a_ref = bufs
        m_old = m_ref[c, :, cols]
        m_new = jnp.maximum(m_old, t_ref[c, :, cols] + shift)
        a_ref[c, :, cols] = jnp.exp2(m_old - m_new)
        p_ref[c, :, cols] = jnp.exp2(s_ref[c, :, cols] - (m_new - shift)).astype(BF16)
        m_ref[c, :, cols] = m_new

    def values(vt, bufs, c, cols):
        p_ref, a_ref = bufs[2], bufs[3]
        acc_ref[c, :, cols] = (a_ref[c, :, cols] * acc_ref[c, :, cols]
                               + jnp.dot(vt, p_ref[c, :, cols], preferred_element_type=F32))

    def block(nxt, cur, prv):
        for c, cols in chunks:
            if nxt is not None:
                kb = k_ref[pl.ds(pl.multiple_of(nxt[0] * tk, tk), tk), :]
                scores(kb, nxt[1], nxt[2], c, cols)
            if cur is not None:
                softmax(cur[0], cur[1], c, cols)
            if prv is not None:
                vt = jnp.concatenate([v_ref[prv[0]], ones_rows], axis=0)
                values(vt, prv[1], c, cols)

    k_left = jnp.maximum(qi - 1, 0)
    k_right = jnp.minimum(qi + 1, nk - 1)
    has_left = qi >= 1
    has_right = qi + 1 < nk

    def near_bias(index, exists):
        def fn(cols):
            bias = b_ref[0, index, :, cols]
            return bias if exists is None else jnp.where(exists, bias, NEG_INF)
        return fn

    n_near = k_right - k_left + 1
    n_slots = nk - 2

    def far(f):
        ki = jnp.minimum(jnp.where(f < k_left, f, f + n_near), nk - 1)
        shift = jnp.where(f < nk - n_near, jnp.where(ki < qi, c_left, c_right), NEG_INF)
        return ki, shift

    far0, shift0 = far(0)
    block((qi, near_bias(1, None), even), None, None)
    block((k_left, near_bias(0, has_left), odd), (0.0, even), None)
    block((k_right, near_bias(2, has_right), even), (0.0, odd), (qi, even))
    block((far0, None, odd), (0.0, even), (k_left, odd))

    def pair(f, k_prev):
        k_a, shift_a = far(f)
        k_b, shift_b = far(f + 1)
        k_c, _ = far(f + 2)
        block((k_b, None, even), (shift_a, odd), (k_prev, even))
        block((k_c, None, odd), (shift_b, even), (k_a, odd))
        return k_b

    def pairs(j, k_prev):
        for i in range(DIFF_PAIRS_PER_TRIP):
            k_prev = pair(2 * (DIFF_PAIRS_PER_TRIP * j + i), k_prev)
        return k_prev

    assert n_slots % (2 * DIFF_PAIRS_PER_TRIP) == 0
    k_last = lax.fori_loop(0, n_slots // (2 * DIFF_PAIRS_PER_TRIP), pairs, k_right)
    block(None, None, (k_last, even))

    lam_init = scal_ref[2 * DIFF_HEADS]
    lq = lq_ref[...]
    lam = (jnp.exp(jnp.sum(lq[0:1] * lq[1:2], axis=-1, keepdims=True))
           - jnp.exp(jnp.sum(lq[2:3] * lq[3:4], axis=-1, keepdims=True)) + lam_init)
    acc0, acc1 = acc_ref[0], acc_ref[1]
    o = (acc0[:two_d] / acc0[two_d:two_d + 1]
         - lam * (acc1[:two_d] / acc1[two_d:two_d + 1]))
    y = o * lax.rsqrt(jnp.mean(o * o, axis=0, keepdims=True) + EPS) * g_ref[...]
    o_ref[...] = jnp.transpose(y * (1.0 - lam_init)).astype(o_ref.dtype)


def _diff(scal, proj, proj_t, bias_t, subln_g, lq):
    S = proj.shape[0]
    tq, tk = DIFF_TQ, DIFF_TK
    nk = S // tk
    assert nk % 2 == 0 and tq == ROW_TILE and tk == ROW_TILE
    two_d = 2 * HEAD_DIM
    kcol = N_KD // LANES
    vrow = DIFF_HEADS
    return pl.pallas_call(
        functools.partial(_diff_kernel, nk=nk),
        grid=(DIFF_HEADS, S // tq),
        in_specs=[
            pl.BlockSpec(memory_space=pltpu.SMEM),
            pl.BlockSpec((1, two_d, tq), lambda h, i: (i, h, 0)),
            pl.BlockSpec((S, LANES), lambda h, i: (0, kcol + h)),
            pl.BlockSpec((nk, two_d, tk), lambda h, i: (0, vrow + h, 0)),
            pl.BlockSpec((1, 3, tk, tq), lambda h, i: (h, 0, 0, 0)),
            _resident((two_d, 1), lambda h, i: (0, 0)),
            _resident((4, HEAD_DIM), lambda h, i: (0, 0)),
        ],
        out_specs=pl.BlockSpec((tq, two_d), lambda h, i: (i, h)),
        out_shape=jax.ShapeDtypeStruct((S, DIFF_HEADS * two_d), BF16),
        scratch_shapes=[
            pltpu.VMEM((two_d, tq), BF16),
            pltpu.VMEM((two_d, tq), BF16),
            pltpu.VMEM((2, tk, tq), F32),
            pltpu.VMEM((2, tk, tq), F32),
            pltpu.VMEM((2, 1, tq), F32),
            pltpu.VMEM((2, 1, tq), F32),
            pltpu.VMEM((2, tk, tq), BF16),
            pltpu.VMEM((2, tk, tq), BF16),
            pltpu.VMEM((2, 1, tq), F32),
            pltpu.VMEM((2, 1, tq), F32),
            pltpu.VMEM((2, 1, tq), F32),
            pltpu.VMEM((2, DIFF_VROWS, tq), F32),
        ],
        compiler_params=_params("parallel", "arbitrary"),
        name="diff_attn",
    )(scal, proj_t, proj, proj_t, bias_t, subln_g, lq)


def _diff_bias_tiles(table):
    tq, tk = DIFF_TQ, DIFF_TK
    tiles = []
    for d in (-1, 0, 1):
        vals = _rel_bias(table, d * tk - (tq - 1), d * tk + tk - 1)
        tiles.append(jnp.swapaxes(_toeplitz(vals, tq, tk), -1, -2))
    return jnp.stack(tiles, axis=1)


def _gqa_kernel(q_ref, kp_ref, kc_ref, kn_ref, vp_ref, vc_ref, vn_ref, b_ref, sink_ref, o_ref, *, n_blocks):
    blk = pl.program_id(1)
    sub = GQA_QB // WINDOW
    kw = 3 * WINDOW
    width = GQA_GROUP * HEAD_DIM
    q = q_ref[...].astype(F32)
    kx = jnp.concatenate([kp_ref[...], kc_ref[...], kn_ref[...]], axis=0)
    vx = jnp.concatenate([vp_ref[...], vc_ref[...], vn_ref[...]], axis=0)
    lane = lax.broadcasted_iota(jnp.int32, (WINDOW, width), 1) // HEAD_DIM
    kcol = lax.broadcasted_iota(jnp.int32, (1, kw), 1)
    sink = sink_ref[0]

    def logits(i):
        qi = q[i * WINDOW:(i + 1) * WINDOW]
        qs = jnp.concatenate([jnp.where(lane == g, qi, 0.0).astype(BF16) for g in range(GQA_GROUP)], axis=0)
        s = lax.dot_general(qs, kx[i * WINDOW:i * WINDOW + kw], (((1,), (1,)), ((), ())),
                            preferred_element_type=F32) + b_ref[0]
        if i == 0:
            s = jnp.where((kcol < WINDOW) & (blk == 0), NEG_INF, s)
        if i == sub - 1:
            s = jnp.where((kcol >= 2 * WINDOW) & (blk == n_blocks - 1), NEG_INF, s)
        return s

    def attend(i, s):
        m = jnp.maximum(jnp.max(s, axis=-1, keepdims=True), sink)
        p = jnp.exp(s - m)
        l = jnp.sum(p, axis=-1, keepdims=True) + jnp.exp(sink - m)
        o = jnp.dot(p.astype(BF16), vx[i * WINDOW:i * WINDOW + kw], preferred_element_type=F32) / l
        out = o[:WINDOW]
        for g in range(1, GQA_GROUP):
            out = jnp.where(lane == g, o[g * WINDOW:(g + 1) * WINDOW], out)
        o_ref[i * WINDOW:(i + 1) * WINDOW, :] = out.astype(o_ref.dtype)

    _staggered(sub, logits, attend)


def _gqa(proj, kx, vx, bias, sink_rows):
    S = proj.shape[0]
    qb, kw = GQA_QB, 3 * WINDOW
    width = GQA_GROUP * HEAD_DIM
    nb = S // qb
    sub = qb // WINDOW
    last = S // WINDOW - 1
    qcol = N_QC // width

    def prev_map(kv, b):
        return (jnp.maximum(b * sub - 1, 0), kv)

    def next_map(kv, b):
        return (jnp.minimum(b * sub + sub, last), kv)

    side = pl.BlockSpec((WINDOW, width), prev_map)
    nxt = pl.BlockSpec((WINDOW, width), next_map)
    cur = pl.BlockSpec((qb, width), lambda kv, b: (b, kv))
    return pl.pallas_call(
        functools.partial(_gqa_kernel, n_blocks=nb),
        grid=(GQA_KV_HEADS, nb),
        in_specs=[
            pl.BlockSpec((qb, width), lambda kv, b: (b, qcol + kv)),
            side, cur, nxt, side, cur, nxt,
            pl.BlockSpec((1, GQA_GROUP * WINDOW, kw), lambda kv, b: (kv, 0, 0)),
            pl.BlockSpec((1, GQA_GROUP * WINDOW, 1), lambda kv, b: (kv, 0, 0)),
        ],
        out_specs=pl.BlockSpec((qb, width), lambda kv, b: (b, kv)),
        out_shape=jax.ShapeDtypeStruct((S, GQA_Q_HEADS * HEAD_DIM), BF16),
        compiler_params=_params("parallel", "arbitrary"),
        name="gqa_attn",
    )(proj, kx, kx, kx, vx, vx, vx, bias, sink_rows)


def _gqa_bias(table):
    qb, kw = WINDOW, 3 * WINDOW
    vals = _rel_bias(table, -WINDOW - (qb - 1), kw - 1 - WINDOW)
    rel = jnp.arange(-WINDOW - (qb - 1), kw - WINDOW, dtype=jnp.int32)
    vals = jnp.where(jnp.abs(rel)[None, :] <= WINDOW, vals, NEG_INF)
    bias = _toeplitz(vals, qb, kw)
    return bias.reshape(GQA_KV_HEADS, GQA_GROUP * qb, kw)


def _merge_kernel(x_ref, ya_ref, yb_ref, yc_ref, g_ref, wg_ref, bg_ref, wb_ref, wo_ref, o_ref):
    x = x_ref[...]
    h = _rms(x, g_ref[...]).astype(BF16)
    gates = _sigmoid(jnp.dot(h, wg_ref[...], preferred_element_type=F32) + bg_ref[...])
    merged = None
    for i, y_ref in enumerate((ya_ref, yb_ref, yc_ref)):
        t = gates[:, i * D_MODEL:(i + 1) * D_MODEL] * jnp.dot(y_ref[...], wb_ref[i], preferred_element_type=F32)
        merged = t if merged is None else merged + t
    o_ref[...] = x + jnp.dot(merged.astype(BF16), wo_ref[...], preferred_element_type=F32)


def _merge(x, ya, yb, yc, g, wg, bg, wb, wo, layer):
    S = x.shape[0]
    tm = ROW_TILE
    branch = pl.BlockSpec((tm, BRANCH_WIDTH), lambda i: (i, 0))
    return pl.pallas_call(
        _merge_kernel,
        grid=(S // tm,),
        in_specs=[
            pl.BlockSpec((tm, D_MODEL), lambda i: (i, 0)),
            branch, branch, branch,
            _resident((1, D_MODEL), lambda i: (0, 0)),
            _resident((None, D_MODEL, N_BRANCHES * D_MODEL), lambda i: (layer, 0, 0)),
            _resident((1, N_BRANCHES * D_MODEL), lambda i: (0, 0)),
            _resident((None, N_BRANCHES, BRANCH_WIDTH, D_MODEL), lambda i: (layer, 0, 0, 0)),
            _resident((None, D_MODEL, D_MODEL), lambda i: (layer, 0, 0)),
        ],
        out_specs=pl.BlockSpec((tm, D_MODEL), lambda i: (i, 0)),
        out_shape=jax.ShapeDtypeStruct((S, D_MODEL), F32),
        compiler_params=_params("parallel"),
        name="merge",
    )(x, ya, yb, yc, g, wg, bg, wb, wo)


def kernel(x, w_in, w_branch, w_gate, b_gate, w_o, norm_g, final_g, ffn_w_gate, ffn_w_up, ffn_w_down,
           na_rpb, diff_lambda, diff_subln_g, gqa_sink, rel_bias_table):
    B, S, _ = x.shape
    assert B == 1 and S % (2 * DIFF_TK) == 0 and S // (NA_BLOCK_ROWS * GRID_W) >= 4
    depth = w_in.shape[0]
    two_d = 2 * HEAD_DIM
    nk = S // DIFF_TK

    col_scale = np.ones((W_IN_COLS,), np.float32)
    for off, width in ((QA_OFF, 512), (QD_OFF, 512), (QC_OFF, 512)):
        col_scale[off:off + width] = HEAD_DIM ** -0.5
    col_scale[QD_OFF:QD_OFF + 512] *= LOG2E
    w_in_s = w_in * col_scale
    w_in_n = jnp.concatenate([w_in_s[:, :, :QD_OFF], w_in_s[:, :, KD_OFF:VD_OFF], w_in_s[:, :, QC_OFF:]],
                             axis=2).astype(BF16)
    w_in_t = jnp.swapaxes(jnp.concatenate([w_in_s[:, :, QD_OFF:KD_OFF], w_in_s[:, :, VD_OFF:QC_OFF]], axis=2),
                          1, 2).astype(BF16)
    w_branch_b = w_branch.astype(BF16)
    w_gate_b = w_gate.astype(BF16)
    w_o_b = w_o.astype(BF16)
    ffn_wg_b = ffn_w_gate.astype(BF16)
    ffn_wu_b = ffn_w_up.astype(BF16)
    ffn_wd_b = ffn_w_down.astype(BF16)

    na_bias = _na_bias_tables(na_rpb, S // GRID_W)
    diff_table = rel_bias_table[:, :DIFF_HEADS]
    diff_bias = _diff_bias_tiles(diff_table) * LOG2E
    far = _rel_bias(diff_table, -T5_MAX_DIST, T5_MAX_DIST) * LOG2E
    gqa_bias = _gqa_bias(rel_bias_table[:, DIFF_HEADS:])
    final_row = final_g.reshape(1, D_MODEL).astype(F32)

    xs = x.reshape(S, D_MODEL)
    for l in range(depth):
        xs = _ffn(xs, norm_g[l, 0].reshape(1, D_MODEL), ffn_wg_b, ffn_wu_b, ffn_wd_b, final_row, l, 0, False)
        g1 = norm_g[l, 1].reshape(1, D_MODEL)
        proj, proj_t = _proj(xs, g1, w_in_n, w_in_t, l)

        ya = _na(proj, na_bias[l])

        lam_init = 0.8 - 0.6 * math.exp(-0.3 * l)
        scal = jnp.concatenate([far[:, 0], far[:, -1], jnp.full((1,), lam_init, F32)])
        yb = _diff(scal, proj, proj_t, diff_bias, diff_subln_g[l].reshape(two_d, 1).astype(F32),
                   diff_lambda[l].astype(F32))

        kx = jnp.tile(proj[:, N_KC:N_KC + 128].reshape(S, GQA_KV_HEADS, 1, HEAD_DIM),
                      (1, 1, GQA_GROUP, 1)).reshape(S, GQA_Q_HEADS * HEAD_DIM)
        vx = jnp.tile(proj[:, N_VC:N_VC + 128].reshape(S, GQA_KV_HEADS, 1, HEAD_DIM),
                      (1, 1, GQA_GROUP, 1)).reshape(S, GQA_Q_HEADS * HEAD_DIM)
        sink_rows = jnp.repeat(gqa_sink[l].astype(F32), WINDOW).reshape(GQA_KV_HEADS, GQA_GROUP * WINDOW, 1)
        yc = _gqa(proj, kx, vx, gqa_bias, sink_rows)

        xs = _merge(xs, ya, yb, yc, g1, w_gate_b, b_gate[l].reshape(1, -1).astype(F32), w_branch_b, w_o_b, l)
        xs = _ffn(xs, norm_g[l, 2].reshape(1, D_MODEL), ffn_wg_b, ffn_wu_b, ffn_wd_b, final_row, l, 1,
                  l == depth - 1)
    return xs.reshape(B, S, D_MODEL)
```

```python
import functools
import math

import jax
import jax.numpy as jnp
import numpy as np
from jax import lax
from jax.experimental import pallas as pl
from jax.experimental.pallas import tpu as pltpu

D_MODEL = 1024
DEPTH = 4
HEAD_DIM = 64
NA_HEADS = 8
DIFF_HEADS = 4
GQA_Q_HEADS = 8
GQA_KV_HEADS = 2
GQA_GROUP = GQA_Q_HEADS // GQA_KV_HEADS
BRANCH_WIDTH = 512
N_BRANCHES = 3
D_FF = 2816
GRID_W = 64
NA_WIN_ROWS = 8
NA_WIN_COLS = 16
WINDOW = 128
T5_BUCKETS = 32
T5_MAX_DIST = 128
NEG_INF = -1e30
EPS = 1e-6

QA_OFF, KA_OFF, VA_OFF = 0, 512, 1024
QD_OFF, KD_OFF, VD_OFF = 1536, 2048, 2560
QC_OFF, KC_OFF, VC_OFF = 3072, 3584, 3712
W_IN_COLS = 3840
N_QA, N_KA, N_VA, N_KD, N_KC = 0, 512, 1024, 1536, 2048
PROJ_N_COLS = 2560
T_QD, T_VD, T_QC, T_VC = 0, 512, 1024, 1536
PROJ_T_ROWS = 1664

LANES = 128
VMEM_LIMIT = 56 * 1024 * 1024

ROW_TILE = 512
FFN_ROW_TILE = 256
NA_BLOCK_ROWS = 4
NA_WIN_BLOCK_ROWS = 12
DIFF_TQ = 512
DIFF_TK = 512
DIFF_CHUNK = 256
DIFF_PAIRS_PER_TRIP = 3
DIFF_VROWS = 2 * HEAD_DIM + 16
LOG2E = math.log2(math.e)
GQA_QB = 512

F32 = jnp.float32
BF16 = jnp.bfloat16


def _resident(shape, index_map):
    return pl.BlockSpec(shape, index_map, pipeline_mode=pl.Buffered(1))


def _params(*sem, flags=None):
    return pltpu.CompilerParams(dimension_semantics=sem, vmem_limit_bytes=VMEM_LIMIT, flags=flags)


def _rms(x, g):
    return x * lax.rsqrt(jnp.mean(x * x, axis=-1, keepdims=True) + EPS) * g


def _sigmoid(z):
    return 1.0 / (1.0 + jnp.exp(-z))


def _ffn_kernel(x_ref, g_ref, wg_ref, wu_ref, wd_ref, fg_ref, o_ref, *, final):
    x = x_ref[...]
    h = _rms(x, g_ref[...]).astype(BF16)
    gate = jnp.dot(h, wg_ref[...], preferred_element_type=F32)
    up = jnp.dot(h, wu_ref[...], preferred_element_type=F32)
    a = (gate * _sigmoid(gate) * up).astype(BF16)
    y = x + 0.5 * jnp.dot(a, wd_ref[...], preferred_element_type=F32)
    if final:
        y = _rms(y, fg_ref[...])
    o_ref[...] = y


def _ffn(x, g, wg, wu, wd, fg, layer, which, final):
    S = x.shape[0]
    tm = FFN_ROW_TILE
    pick = lambda i: (layer, which, 0, 0)
    return pl.pallas_call(
        functools.partial(_ffn_kernel, final=final),
        grid=(S // tm,),
        in_specs=[
            pl.BlockSpec((tm, D_MODEL), lambda i: (i, 0)),
            _resident((1, D_MODEL), lambda i: (0, 0)),
            _resident((None, None, D_MODEL, D_FF), pick),
            _resident((None, None, D_MODEL, D_FF), pick),
            _resident((None, None, D_FF, D_MODEL), pick),
            _resident((1, D_MODEL), lambda i: (0, 0)),
        ],
        out_specs=pl.BlockSpec((tm, D_MODEL), lambda i: (i, 0)),
        out_shape=jax.ShapeDtypeStruct((S, D_MODEL), F32),
        compiler_params=_params("parallel"),
        name="ffn",
    )(x, g, wg, wu, wd, fg)


def _proj_kernel(x_ref, g_ref, wn_ref, wt_ref, on_ref, ot_ref):
    h = _rms(x_ref[...], g_ref[...]).astype(BF16)
    on_ref[...] = jnp.dot(h, wn_ref[...], preferred_element_type=F32).astype(BF16)
    ot_ref[0] = lax.dot_general(wt_ref[...], h, (((1,), (1,)), ((), ())),
                                preferred_element_type=F32).astype(BF16)


def _proj(x, g, wn, wt, layer):
    S = x.shape[0]
    tm = ROW_TILE
    return pl.pallas_call(
        _proj_kernel,
        grid=(S // tm,),
        in_specs=[
            pl.BlockSpec((tm, D_MODEL), lambda i: (i, 0)),
            _resident((1, D_MODEL), lambda i: (0, 0)),
            _resident((None, D_MODEL, PROJ_N_COLS), lambda i: (layer, 0, 0)),
            _resident((None, PROJ_T_ROWS, D_MODEL), lambda i: (layer, 0, 0)),
        ],
        out_specs=[
            pl.BlockSpec((tm, PROJ_N_COLS), lambda i: (i, 0)),
            pl.BlockSpec((1, PROJ_T_ROWS, tm), lambda i: (i, 0, 0)),
        ],
        out_shape=[
            jax.ShapeDtypeStruct((S, PROJ_N_COLS), BF16),
            jax.ShapeDtypeStruct((S // tm, PROJ_T_ROWS, tm), BF16),
        ],
        compiler_params=_params("parallel"),
        name="proj",
    )(x, g, wn, wt)


def _staggered(n, stage_a, stage_b):
    cur = stage_a(0)
    for i in range(n):
        nxt = stage_a(i + 1) if i + 1 < n else None
        stage_b(i, cur)
        cur = nxt


def _na_kernel(q_ref, k0_ref, k1_ref, k2_ref, v0_ref, v1_ref, v2_ref, b_ref, o_ref):
    nq = NA_BLOCK_ROWS * GRID_W
    q = q_ref[...].astype(F32)
    k3 = jnp.concatenate([k0_ref[...], k1_ref[...], k2_ref[...]], axis=0)
    v3 = jnp.concatenate([v0_ref[...], v1_ref[...], v2_ref[...]], axis=0)
    first = lax.broadcasted_iota(jnp.int32, (nq, LANES), 1) < HEAD_DIM
    outs = [None] * NA_HEADS

    def logits(head):
        cols = slice((head // 2) * LANES, (head // 2 + 1) * LANES)
        keep = first if head % 2 == 0 else jnp.logical_not(first)
        qm = jnp.where(keep, q[:, cols], 0.0).astype(BF16)
        s = lax.dot_general(qm, k3[:, cols], (((1,), (1,)), ((), ())), preferred_element_type=F32)
        return s + b_ref[0, head]

    def attend(head, s):
        cols = slice((head // 2) * LANES, (head // 2 + 1) * LANES)
        m = jnp.max(s, axis=-1, keepdims=True)
        p = jnp.exp(s - m)
        l = jnp.sum(p, axis=-1, keepdims=True)
        outs[head] = jnp.dot(p.astype(BF16), v3[:, cols], preferred_element_type=F32) / l

    _staggered(NA_HEADS, logits, attend)
    for pair in range(NA_HEADS // 2):
        o_ref[:, pair * LANES:(pair + 1) * LANES] = jnp.where(
            first, outs[2 * pair], outs[2 * pair + 1]).astype(o_ref.dtype)


def _na(proj, bias):
    S = proj.shape[0]
    nq = NA_BLOCK_ROWS * GRID_W
    nb = S // nq
    width = NA_HEADS * HEAD_DIM
    qcol, kcol, vcol = N_QA // width, N_KA // width, N_VA // width

    def window(col, j):
        return pl.BlockSpec((nq, width), lambda b: (jnp.clip(b - 1, 0, nb - 3) + j, col))

    return pl.pallas_call(
        _na_kernel,
        grid=(nb,),
        in_specs=[
            pl.BlockSpec((nq, width), lambda b: (b, qcol)),
            window(kcol, 0), window(kcol, 1), window(kcol, 2),
            window(vcol, 0), window(vcol, 1), window(vcol, 2),
            pl.BlockSpec((1, NA_HEADS, nq, 3 * nq),
                         lambda b: (jnp.where(b == 0, 0, jnp.where(b == nb - 1, 2, 1)), 0, 0, 0)),
        ],
        out_specs=pl.BlockSpec((nq, width), lambda b: (b, 0)),
        out_shape=jax.ShapeDtypeStruct((S, width), BF16),
        compiler_params=_params("parallel"),
        name="na_attn",
    )(proj, proj, proj, proj, proj, proj, proj, bias)


def _na_bias_tables(rpb, rows):
    br, wr = NA_BLOCK_ROWS, NA_WIN_BLOCK_ROWS
    half = NA_WIN_ROWS // 2
    c = np.arange(GRID_W)
    cs = np.clip(c - NA_WIN_COLS // 2, 0, GRID_W - NA_WIN_COLS)
    col_ok = (c[None, :] >= cs[:, None]) & (c[None, :] < cs[:, None] + NA_WIN_COLS)
    pad = GRID_W - NA_WIN_COLS
    vals = jnp.pad(rpb.astype(F32), ((0, 0), (0, 0), (0, 0), (pad, pad)))
    t_all = jnp.where(col_ok, _toeplitz(vals, GRID_W, GRID_W), NEG_INF)
    L, H = rpb.shape[0], rpb.shape[1]
    types = []
    for b in (0, 1, rows // br - 1):
        win0 = int(np.clip(b * br - half, 0, rows - wr))
        row_blocks = []
        for j in range(br):
            r = b * br + j
            rs = int(np.clip(r - half, 0, rows - NA_WIN_ROWS))
            w_lo, dr_lo = rs - win0, rs - r + (NA_WIN_ROWS - 1)
            piece = jnp.moveaxis(t_all[:, :, dr_lo:dr_lo + NA_WIN_ROWS], 2, 3)
            piece = piece.reshape(L, H, GRID_W, NA_WIN_ROWS * GRID_W)
            row_blocks.append(jnp.pad(
                piece, ((0, 0), (0, 0), (0, 0), (w_lo * GRID_W, (wr - NA_WIN_ROWS - w_lo) * GRID_W)),
                constant_values=NEG_INF))
        types.append(jnp.stack(row_blocks, axis=2).reshape(L, H, br * GRID_W, wr * GRID_W))
    return jnp.stack(types, axis=1)


def _t5_bucket(rel):
    half = T5_BUCKETS // 2
    max_exact = half // 2
    ret = (rel > 0).astype(jnp.int32) * half
    n = jnp.abs(rel)
    nf = jnp.maximum(n, 1).astype(jnp.float32)
    large = max_exact + (jnp.log(nf / max_exact) / math.log(T5_MAX_DIST / max_exact)
                         * (half - max_exact)).astype(jnp.int32)
    large = jnp.minimum(large, half - 1)
    return ret + jnp.where(n < max_exact, n, large)


def _toeplitz(vals, n_rows, n_cols):
    length = n_rows + n_cols - 1
    lead = vals.shape[:-1]
    w = jnp.concatenate([vals, jnp.zeros(lead + (1,), vals.dtype)], axis=-1)
    flat = jnp.tile(w, (1,) * len(lead) + (n_rows,))[..., : n_rows * length]
    return flat.reshape(lead + (n_rows, length))[..., n_rows - 1:]


def _rel_bias(table, lo, hi):
    rel = jnp.arange(lo, hi + 1, dtype=jnp.int32)
    return jnp.transpose(table.astype(F32)[_t5_bucket(rel)], (1, 0))


def _diff_kernel(scal_ref, q_ref, k_ref, v_ref, b_ref, g_ref, lq_ref, o_ref,
                 qa_ref, qb_ref, s0_ref, s1_ref, t0_ref, t1_ref, p0_ref, p1_ref, a0_ref, a1_ref,
                 m_ref, acc_ref, *, nk):
    h = pl.program_id(0)
    qi = pl.program_id(1)
    tq, tk = DIFF_TQ, DIFF_TK
    two_d = 2 * HEAD_DIM

    q = q_ref[0].astype(F32)
    row = lax.broadcasted_iota(jnp.int32, (two_d, tq), 0)
    qa_ref[...] = jnp.where(row < HEAD_DIM, q, 0.0).astype(BF16)
    qb_ref[...] = jnp.where(row >= HEAD_DIM, q, 0.0).astype(BF16)
    m_ref[...] = jnp.full(m_ref.shape, NEG_INF, F32)
    acc_ref[...] = jnp.zeros(acc_ref.shape, F32)
    c_left = scal_ref[h]
    c_right = scal_ref[DIFF_HEADS + h]
    ones_rows = jnp.where(lax.broadcasted_iota(jnp.int32, (DIFF_VROWS - two_d, tk), 0) == 0,
                          1.0, 0.0).astype(BF16)

    chunks = [(c, slice(j, j + DIFF_CHUNK)) for c in range(2) for j in range(0, tq, DIFF_CHUNK)]
    qm_refs = (qa_ref, qb_ref)
    even = (s0_ref, t0_ref, p0_ref, a0_ref)
    odd = (s1_ref, t1_ref, p1_ref, a1_ref)

    def scores(kb, bias_fn, bufs, c, cols):
        s_ref, t_ref = bufs[0], bufs[1]
        s = jnp.dot(kb, qm_refs[c][:, cols], preferred_element_type=F32)
        if bias_fn is not None:
            s = s + bias_fn(cols)
        s_ref[c, :, cols] = s
        t_ref[c, :, cols] = jnp.max(s, axis=0, keepdims=True)

    def softmax(shift, bufs, c, cols):
        s_ref, t_ref, p_ref, a_ref = bufs
        m_old = m_ref[c, :, cols]
        m_new = jnp.maximum(m_old, t_ref[c, :, cols] + shift)
        a_ref[c, :, cols] = jnp.exp2(m_old - m_new)
        p_ref[c, :, cols] = jnp.exp2(s_ref[c, :, cols] - (m_new - shift)).astype(BF16)
        m_ref[c, :, cols] = m_new

    def values(vt, bufs, c, cols):
        p_ref, a_ref = bufs[2], bufs[3]
        acc_ref[c, :, cols] = (a_ref[c, :, cols] * acc_ref[c, :, cols]
                               + jnp.dot(vt, p_ref[c, :, cols], preferred_element_type=F32))

    def block(nxt, cur, prv):
        for c, cols in chunks:
            if nxt is not None:
                kb = k_ref[pl.ds(pl.multiple_of(nxt[0] * tk, tk), tk), :]
                scores(kb, nxt[1], nxt[2], c, cols)
            if cur is not None:
                softmax(cur[0], cur[1], c, cols)
            if prv is not None:
                vt = jnp.concatenate([v_ref[prv[0]], ones_rows], axis=0)
                values(vt, prv[1], c, cols)

    k_left = jnp.maximum(qi - 1, 0)
    k_right = jnp.minimum(qi + 1, nk - 1)
    has_left = qi >= 1
    has_right = qi + 1 < nk

    def near_bias(index, exists):
        def fn(cols):
            bias = b_ref[0, index, :, cols]
            return bias if exists is None else jnp.where(exists, bias, NEG_INF)
        return fn

    n_near = k_right - k_left + 1
    n_slots = nk - 2

    def far(f):
        ki = jnp.minimum(jnp.where(f < k_left, f, f + n_near), nk - 1)
        shift = jnp.where(f < nk - n_near, jnp.where(ki < qi, c_left, c_right), NEG_INF)
        return ki, shift

    far0, shift0 = far(0)
    block((qi, near_bias(1, None), even), None, None)
    block((k_left, near_bias(0, has_left), odd), (0.0, even), None)
    block((k_right, near_bias(2, has_right), even), (0.0, odd), (qi, even))
    block((far0, None, odd), (0.0, even), (k_left, odd))

    def pair(f, k_prev):
        k_a, shift_a = far(f)
        k_b, shift_b = far(f + 1)
        k_c, _ = far(f + 2)
        block((k_b, None, even), (shift_a, odd), (k_prev, even))
        block((k_c, None, odd), (shift_b, even), (k_a, odd))
        return k_b

    def pairs(j, k_prev):
        for i in range(DIFF_PAIRS_PER_TRIP):
            k_prev = pair(2 * (DIFF_PAIRS_PER_TRIP * j + i), k_prev)
        return k_prev

    assert n_slots % (2 * DIFF_PAIRS_PER_TRIP) == 0
    k_last = lax.fori_loop(0, n_slots // (2 * DIFF_PAIRS_PER_TRIP), pairs, k_right)
    block(None, None, (k_last, even))

    lam_init = scal_ref[2 * DIFF_HEADS]
    lq = lq_ref[...]
    lam = (jnp.exp(jnp.sum(lq[0:1] * lq[1:2], axis=-1, keepdims=True))
           - jnp.exp(jnp.sum(lq[2:3] * lq[3:4], axis=-1, keepdims=True)) + lam_init)
    acc0, acc1 = acc_ref[0], acc_ref[1]
    o = (acc0[:two_d] / acc0[two_d:two_d + 1]
         - lam * (acc1[:two_d] / acc1[two_d:two_d + 1]))
    y = o * lax.rsqrt(jnp.mean(o * o, axis=0, keepdims=True) + EPS) * g_ref[...]
    o_ref[...] = jnp.transpose(y * (1.0 - lam_init)).astype(o_ref.dtype)


def _diff(scal, proj, proj_t, bias_t, subln_g, lq):
    S = proj.shape[0]
    tq, tk = DIFF_TQ, DIFF_TK
    nk = S // tk
    assert nk % 2 == 0 and tq == ROW_TILE and tk == ROW_TILE
    two_d = 2 * HEAD_DIM
    kcol = N_KD // LANES
    qrow, vrow = T_QD // two_d, T_VD // two_d
    return pl.pallas_call(
        functools.partial(_diff_kernel, nk=nk),
        grid=(DIFF_HEADS, S // tq),
        in_specs=[
            pl.BlockSpec(memory_space=pltpu.SMEM),
            pl.BlockSpec((1, two_d, tq), lambda h, i: (i, qrow + h, 0)),
            pl.BlockSpec((S, LANES), lambda h, i: (0, kcol + h)),
            pl.BlockSpec((nk, two_d, tk), lambda h, i: (0, vrow + h, 0)),
            pl.BlockSpec((1, 3, tk, tq), lambda h, i: (h, 0, 0, 0)),
            _resident((two_d, 1), lambda h, i: (0, 0)),
            _resident((4, HEAD_DIM), lambda h, i: (0, 0)),
        ],
        out_specs=pl.BlockSpec((tq, two_d), lambda h, i: (i, h)),
        out_shape=jax.ShapeDtypeStruct((S, DIFF_HEADS * two_d), BF16),
        scratch_shapes=[
            pltpu.VMEM((two_d, tq), BF16),
            pltpu.VMEM((two_d, tq), BF16),
            pltpu.VMEM((2, tk, tq), F32),
            pltpu.VMEM((2, tk, tq), F32),
            pltpu.VMEM((2, 1, tq), F32),
            pltpu.VMEM((2, 1, tq), F32),
            pltpu.VMEM((2, tk, tq), BF16),
            pltpu.VMEM((2, tk, tq), BF16),
            pltpu.VMEM((2, 1, tq), F32),
            pltpu.VMEM((2, 1, tq), F32),
            pltpu.VMEM((2, 1, tq), F32),
            pltpu.VMEM((2, DIFF_VROWS, tq), F32),
        ],
        compiler_params=_params("parallel", "arbitrary"),
        name="diff_attn",
    )(scal, proj_t, proj, proj_t, bias_t, subln_g, lq)


def _diff_bias_tiles(table):
    tq, tk = DIFF_TQ, DIFF_TK
    tiles = []
    for d in (-1, 0, 1):
        vals = _rel_bias(table, d * tk - (tq - 1), d * tk + tk - 1)
        tiles.append(jnp.swapaxes(_toeplitz(vals, tq, tk), -1, -2))
    return jnp.stack(tiles, axis=1)


def _gqa_kernel(q_ref, kp_ref, kc_ref, kn_ref, vp_ref, vc_ref, vn_ref, b_ref, sink_ref, o_ref, *, n_blocks):
    blk = pl.program_id(1)
    sub = GQA_QB // WINDOW
    kw = 3 * WINDOW
    width = GQA_GROUP * HEAD_DIM
    qt = q_ref[0].astype(F32)
    kx = jnp.concatenate([kp_ref[...], kc_ref[...], kn_ref[...]], axis=0)
    vt = jnp.concatenate([vp_ref[0], vc_ref[0], vn_ref[0]], axis=1)
    row_head = lax.broadcasted_iota(jnp.int32, (width, WINDOW), 0) // HEAD_DIM
    krow = lax.broadcasted_iota(jnp.int32, (kw, 1), 0)
    ones_rows = jnp.where(lax.broadcasted_iota(jnp.int32, (16, kw), 0) == 0, 1.0, 0.0).astype(BF16)
    sink = sink_ref[0]
    outs = [None] * sub

    def logits(i):
        qi = qt[:, i * WINDOW:(i + 1) * WINDOW]
        qs = jnp.concatenate([jnp.where(row_head == g, qi, 0.0).astype(BF16) for g in range(GQA_GROUP)], axis=1)
        s = jnp.dot(kx[i * WINDOW:i * WINDOW + kw], qs, preferred_element_type=F32) + b_ref[0]
        if i == 0:
            s = jnp.where((krow < WINDOW) & (blk == 0), NEG_INF, s)
        if i == sub - 1:
            s = jnp.where((krow >= 2 * WINDOW) & (blk == n_blocks - 1), NEG_INF, s)
        return s

    def attend(i, s):
        m = jnp.maximum(jnp.max(s, axis=0, keepdims=True), sink)
        p = jnp.exp(s - m).astype(BF16)
        v_ext = jnp.concatenate([vt[:, i * WINDOW:i * WINDOW + kw], ones_rows], axis=0)
        o = jnp.dot(v_ext, p, preferred_element_type=F32)
        o = o[:HEAD_DIM] / (o[HEAD_DIM:HEAD_DIM + 1] + jnp.exp(sink - m))
        outs[i] = jnp.concatenate([o[:, g * WINDOW:(g + 1) * WINDOW] for g in range(GQA_GROUP)], axis=0)

    _staggered(sub, logits, attend)
    o_ref[...] = jnp.transpose(jnp.concatenate(outs, axis=1)).astype(o_ref.dtype)


def _gqa(proj, proj_t, bias_t, sink_lanes):
    S = proj.shape[0]
    qb, kw = GQA_QB, 3 * WINDOW
    assert qb == ROW_TILE
    width = GQA_GROUP * HEAD_DIM
    nb = S // qb
    sub = qb // WINDOW
    last = S // WINDOW - 1
    kcol = N_KC // width
    qrow = T_QC // width
    vrow = T_VC // HEAD_DIM

    def prev_blk(b):
        return jnp.maximum(b * sub - 1, 0)

    def next_blk(b):
        return jnp.minimum(b * sub + sub, last)

    return pl.pallas_call(
        functools.partial(_gqa_kernel, n_blocks=nb),
        grid=(GQA_KV_HEADS, nb),
        in_specs=[
            pl.BlockSpec((1, width, qb), lambda kv, b: (b, qrow + kv, 0)),
            pl.BlockSpec((WINDOW, width), lambda kv, b: (prev_blk(b), kcol + kv)),
            pl.BlockSpec((qb, width), lambda kv, b: (b, kcol + kv)),
            pl.BlockSpec((WINDOW, width), lambda kv, b: (next_blk(b), kcol + kv)),
            pl.BlockSpec((1, HEAD_DIM, WINDOW), lambda kv, b: (prev_blk(b) // sub, vrow + kv, prev_blk(b) % sub)),
            pl.BlockSpec((1, HEAD_DIM, qb), lambda kv, b: (b, vrow + kv, 0)),
            pl.BlockSpec((1, HEAD_DIM, WINDOW), lambda kv, b: (next_blk(b) // sub, vrow + kv, next_blk(b) % sub)),
            pl.BlockSpec((1, kw, GQA_GROUP * WINDOW), lambda kv, b: (kv, 0, 0)),
            pl.BlockSpec((1, 1, GQA_GROUP * WINDOW), lambda kv, b: (kv, 0, 0)),
        ],
        out_specs=pl.BlockSpec((qb, width), lambda kv, b: (b, kv)),
        out_shape=jax.ShapeDtypeStruct((S, GQA_Q_HEADS * HEAD_DIM), BF16),
        compiler_params=_params("parallel", "arbitrary"),
        name="gqa_attn",
    )(proj_t, proj, proj, proj, proj_t, proj_t, proj_t, bias_t, sink_lanes)


def _gqa_bias(table):
    qb, kw = WINDOW, 3 * WINDOW
    vals = _rel_bias(table, -WINDOW - (qb - 1), kw - 1 - WINDOW)
    rel = jnp.arange(-WINDOW - (qb - 1), kw - WINDOW, dtype=jnp.int32)
    vals = jnp.where(jnp.abs(rel)[None, :] <= WINDOW, vals, NEG_INF)
    bias = _toeplitz(vals, qb, kw)
    bias = bias.reshape(GQA_KV_HEADS, GQA_GROUP, qb, kw)
    return jnp.transpose(bias, (0, 3, 1, 2)).reshape(GQA_KV_HEADS, kw, GQA_GROUP * qb)


def _merge_kernel(x_ref, ya_ref, yb_ref, yc_ref, g_ref, wg_ref, bg_ref, wb_ref, wo_ref, o_ref):
    x = x_ref[...]
    h = _rms(x, g_ref[...]).astype(BF16)
    gates = _sigmoid(jnp.dot(h, wg_ref[...], preferred_element_type=F32) + bg_ref[...])
    merged = None
    for i, y_ref in enumerate((ya_ref, yb_ref, yc_ref)):
        t = gates[:, i * D_MODEL:(i + 1) * D_MODEL] * jnp.dot(y_ref[...], wb_ref[i], preferred_element_type=F32)
        merged = t if merged is None else merged + t
    o_ref[...] = x + jnp.dot(merged.astype(BF16), wo_ref[...], preferred_element_type=F32)


def _merge(x, ya, yb, yc, g, wg, bg, wb, wo, layer):
    S = x.shape[0]
    tm = ROW_TILE
    branch = pl.BlockSpec((tm, BRANCH_WIDTH), lambda i: (i, 0))
    return pl.pallas_call(
        _merge_kernel,
        grid=(S // tm,),
        in_specs=[
            pl.BlockSpec((tm, D_MODEL), lambda i: (i, 0)),
            branch, branch, branch,
            _resident((1, D_MODEL), lambda i: (0, 0)),
            _resident((None, D_MODEL, N_BRANCHES * D_MODEL), lambda i: (layer, 0, 0)),
            _resident((1, N_BRANCHES * D_MODEL), lambda i: (0, 0)),
            _resident((None, N_BRANCHES, BRANCH_WIDTH, D_MODEL), lambda i: (layer, 0, 0, 0)),
            _resident((None, D_MODEL, D_MODEL), lambda i: (layer, 0, 0)),
        ],
        out_specs=pl.BlockSpec((tm, D_MODEL), lambda i: (i, 0)),
        out_shape=jax.ShapeDtypeStruct((S, D_MODEL), F32),
        compiler_params=_params("parallel"),
        name="merge",
    )(x, ya, yb, yc, g, wg, bg, wb, wo)


def kernel(x, w_in, w_branch, w_gate, b_gate, w_o, norm_g, final_g, ffn_w_gate, ffn_w_up, ffn_w_down,
           na_rpb, diff_lambda, diff_subln_g, gqa_sink, rel_bias_table):
    B, S, _ = x.shape
    assert B == 1 and S % (2 * DIFF_TK) == 0 and S // (NA_BLOCK_ROWS * GRID_W) >= 4
    depth = w_in.shape[0]
    two_d = 2 * HEAD_DIM
    nk = S // DIFF_TK

    col_scale = np.ones((W_IN_COLS,), np.float32)
    for off, width in ((QA_OFF, 512), (QD_OFF, 512), (QC_OFF, 512)):
        col_scale[off:off + width] = HEAD_DIM ** -0.5
    col_scale[QD_OFF:QD_OFF + 512] *= LOG2E
    w_in_s = w_in * col_scale
    w_kc_tiled = jnp.tile(w_in_s[:, :, KC_OFF:VC_OFF].reshape(depth, D_MODEL, GQA_KV_HEADS, 1, HEAD_DIM),
                          (1, 1, 1, GQA_GROUP, 1)).reshape(depth, D_MODEL, GQA_Q_HEADS * HEAD_DIM)
    w_in_n = jnp.concatenate([w_in_s[:, :, :QD_OFF], w_in_s[:, :, KD_OFF:VD_OFF], w_kc_tiled], axis=2).astype(BF16)
    w_in_t = jnp.swapaxes(jnp.concatenate(
        [w_in_s[:, :, QD_OFF:KD_OFF], w_in_s[:, :, VD_OFF:QC_OFF], w_in_s[:, :, QC_OFF:KC_OFF],
         w_in_s[:, :, VC_OFF:]], axis=2), 1, 2).astype(BF16)
    w_branch_b = w_branch.astype(BF16)
    w_gate_b = w_gate.astype(BF16)
    w_o_b = w_o.astype(BF16)
    ffn_wg_b = ffn_w_gate.astype(BF16)
    ffn_wu_b = ffn_w_up.astype(BF16)
    ffn_wd_b = ffn_w_down.astype(BF16)

    na_bias = _na_bias_tables(na_rpb, S // GRID_W)
    diff_table = rel_bias_table[:, :DIFF_HEADS]
    diff_bias = _diff_bias_tiles(diff_table) * LOG2E
    far = _rel_bias(diff_table, -T5_MAX_DIST, T5_MAX_DIST) * LOG2E
    gqa_bias = _gqa_bias(rel_bias_table[:, DIFF_HEADS:])
    final_row = final_g.reshape(1, D_MODEL).astype(F32)

    xs = x.reshape(S, D_MODEL)
    for l in range(depth):
        xs = _ffn(xs, norm_g[l, 0].reshape(1, D_MODEL), ffn_wg_b, ffn_wu_b, ffn_wd_b, final_row, l, 0, False)
        g1 = norm_g[l, 1].reshape(1, D_MODEL)
        proj, proj_t = _proj(xs, g1, w_in_n, w_in_t, l)

        ya = _na(proj, na_bias[l])

        lam_init = 0.8 - 0.6 * math.exp(-0.3 * l)
        scal = jnp.concatenate([far[:, 0], far[:, -1], jnp.full((1,), lam_init, F32)])
        yb = _diff(scal, proj, proj_t, diff_bias, diff_subln_g[l].reshape(two_d, 1).astype(F32),
                   diff_lambda[l].astype(F32))

        sink_lanes = jnp.repeat(gqa_sink[l].astype(F32), WINDOW).reshape(GQA_KV_HEADS, 1, GQA_GROUP * WINDOW)
        yc = _gqa(proj, proj_t, gqa_bias, sink_lanes)

        xs = _merge(xs, ya, yb, yc, g1, w_gate_b, b_gate[l].reshape(1, -1).astype(F32), w_branch_b, w_o_b, l)
        xs = _ffn(xs, norm_g[l, 2].reshape(1, D_MODEL), ffn_wg_b, ffn_wu_b, ffn_wd_b, final_row, l, 1,
                  l == depth - 1)
    return xs.reshape(B, S, D_MODEL)
```

```python
import functools
import math

import jax
import jax.numpy as jnp
import numpy as np
from jax import lax
from jax.experimental import pallas as pl
from jax.experimental.pallas import tpu as pltpu

D_MODEL = 1024
DEPTH = 4
HEAD_DIM = 64
NA_HEADS = 8
DIFF_HEADS = 4
GQA_Q_HEADS = 8
GQA_KV_HEADS = 2
GQA_GROUP = GQA_Q_HEADS // GQA_KV_HEADS
BRANCH_WIDTH = 512
N_BRANCHES = 3
D_FF = 2816
GRID_W = 64
NA_WIN_ROWS = 8
NA_WIN_COLS = 16
WINDOW = 128
T5_BUCKETS = 32
T5_MAX_DIST = 128
NEG_INF = -1e30
EPS = 1e-6

QA_OFF, KA_OFF, VA_OFF = 0, 512, 1024
QD_OFF, KD_OFF, VD_OFF = 1536, 2048, 2560
QC_OFF, KC_OFF, VC_OFF = 3072, 3584, 3712
W_IN_COLS = 3840
N_QA, N_KA, N_VA, N_KD, N_KC = 0, 512, 1024, 1536, 2048
PROJ_N_COLS = 2560
T_QD, T_VD, T_QC, T_VC = 0, 512, 1024, 1536
PROJ_T_ROWS = 1664

LANES = 128
VMEM_LIMIT = 56 * 1024 * 1024

ROW_TILE = 512
FFN_ROW_TILE = 256
NA_BLOCK_ROWS = 4
NA_WIN_BLOCK_ROWS = 12
DIFF_TQ = 512
DIFF_TK = 512
DIFF_CHUNK = 256
DIFF_PAIRS_PER_TRIP = 5
DIFF_VROWS = 2 * HEAD_DIM + 16
LOG2E = math.log2(math.e)
GQA_QB = 512

F32 = jnp.float32
BF16 = jnp.bfloat16


def _resident(shape, index_map):
    return pl.BlockSpec(shape, index_map, pipeline_mode=pl.Buffered(1))


def _params(*sem, flags=None):
    return pltpu.CompilerParams(dimension_semantics=sem, vmem_limit_bytes=VMEM_LIMIT, flags=flags)


def _rms(x, g):
    return x * lax.rsqrt(jnp.mean(x * x, axis=-1, keepdims=True) + EPS) * g


def _sigmoid(z):
    return 1.0 / (1.0 + jnp.exp(-z))


def _ffn_kernel(x_ref, g_ref, wg_ref, wu_ref, wd_ref, fg_ref, o_ref, *, final):
    x = x_ref[...]
    h = _rms(x, g_ref[...]).astype(BF16)
    gate = jnp.dot(h, wg_ref[...], preferred_element_type=F32)
    up = jnp.dot(h, wu_ref[...], preferred_element_type=F32)
    a = (gate * _sigmoid(gate) * up).astype(BF16)
    y = x + 0.5 * jnp.dot(a, wd_ref[...], preferred_element_type=F32)
    if final:
        y = _rms(y, fg_ref[...])
    o_ref[...] = y


def _ffn(x, g, wg, wu, wd, fg, layer, which, final):
    S = x.shape[0]
    tm = FFN_ROW_TILE
    pick = lambda i: (layer, which, 0, 0)
    return pl.pallas_call(
        functools.partial(_ffn_kernel, final=final),
        grid=(S // tm,),
        in_specs=[
            pl.BlockSpec((tm, D_MODEL), lambda i: (i, 0)),
            _resident((1, D_MODEL), lambda i: (0, 0)),
            _resident((None, None, D_MODEL, D_FF), pick),
            _resident((None, None, D_MODEL, D_FF), pick),
            _resident((None, None, D_FF, D_MODEL), pick),
            _resident((1, D_MODEL), lambda i: (0, 0)),
        ],
        out_specs=pl.BlockSpec((tm, D_MODEL), lambda i: (i, 0)),
        out_shape=jax.ShapeDtypeStruct((S, D_MODEL), F32),
        compiler_params=_params("parallel"),
        name="ffn",
    )(x, g, wg, wu, wd, fg)


def _proj_kernel(x_ref, g_ref, wn_ref, wt_ref, on_ref, ot_ref):
    h = _rms(x_ref[...], g_ref[...]).astype(BF16)
    on_ref[...] = jnp.dot(h, wn_ref[...], preferred_element_type=F32).astype(BF16)
    ot_ref[0] = lax.dot_general(wt_ref[...], h, (((1,), (1,)), ((), ())),
                                preferred_element_type=F32).astype(BF16)


def _proj(x, g, wn, wt, layer):
    S = x.shape[0]
    tm = ROW_TILE
    return pl.pallas_call(
        _proj_kernel,
        grid=(S // tm,),
        in_specs=[
            pl.BlockSpec((tm, D_MODEL), lambda i: (i, 0)),
            _resident((1, D_MODEL), lambda i: (0, 0)),
            _resident((None, D_MODEL, PROJ_N_COLS), lambda i: (layer, 0, 0)),
            _resident((None, PROJ_T_ROWS, D_MODEL), lambda i: (layer, 0, 0)),
        ],
        out_specs=[
            pl.BlockSpec((tm, PROJ_N_COLS), lambda i: (i, 0)),
            pl.BlockSpec((1, PROJ_T_ROWS, tm), lambda i: (i, 0, 0)),
        ],
        out_shape=[
            jax.ShapeDtypeStruct((S, PROJ_N_COLS), BF16),
            jax.ShapeDtypeStruct((S // tm, PROJ_T_ROWS, tm), BF16),
        ],
        compiler_params=_params("parallel"),
        name="proj",
    )(x, g, wn, wt)


def _staggered(n, stage_a, stage_b):
    cur = stage_a(0)
    for i in range(n):
        nxt = stage_a(i + 1) if i + 1 < n else None
        stage_b(i, cur)
        cur = nxt


def _na_kernel(q_ref, k0_ref, k1_ref, k2_ref, v0_ref, v1_ref, v2_ref, b_ref, o_ref):
    nq = NA_BLOCK_ROWS * GRID_W
    q = q_ref[...].astype(F32)
    k3 = jnp.concatenate([k0_ref[...], k1_ref[...], k2_ref[...]], axis=0)
    v3 = jnp.concatenate([v0_ref[...], v1_ref[...], v2_ref[...]], axis=0)
    first = lax.broadcasted_iota(jnp.int32, (nq, LANES), 1) < HEAD_DIM
    outs = [None] * NA_HEADS

    def logits(head):
        cols = slice((head // 2) * LANES, (head // 2 + 1) * LANES)
        keep = first if head % 2 == 0 else jnp.logical_not(first)
        qm = jnp.where(keep, q[:, cols], 0.0).astype(BF16)
        s = lax.dot_general(qm, k3[:, cols], (((1,), (1,)), ((), ())), preferred_element_type=F32)
        return s + b_ref[0, head]

    def attend(head, s):
        cols = slice((head // 2) * LANES, (head // 2 + 1) * LANES)
        m = jnp.max(s, axis=-1, keepdims=True)
        p = jnp.exp2(s - m)
        l = jnp.sum(p, axis=-1, keepdims=True)
        outs[head] = jnp.dot(p.astype(BF16), v3[:, cols], preferred_element_type=F32) / l

    _staggered(NA_HEADS, logits, attend)
    for pair in range(NA_HEADS // 2):
        o_ref[:, pair * LANES:(pair + 1) * LANES] = jnp.where(
            first, outs[2 * pair], outs[2 * pair + 1]).astype(o_ref.dtype)


def _na(proj, bias, layer):
    S = proj.shape[0]
    nq = NA_BLOCK_ROWS * GRID_W
    nb = S // nq
    width = NA_HEADS * HEAD_DIM
    qcol, kcol, vcol = N_QA // width, N_KA // width, N_VA // width

    def window(col, j):
        return pl.BlockSpec((nq, width), lambda b: (jnp.clip(b - 1, 0, nb - 3) + j, col))

    return pl.pallas_call(
        _na_kernel,
        grid=(nb,),
        in_specs=[
            pl.BlockSpec((nq, width), lambda b: (b, qcol)),
            window(kcol, 0), window(kcol, 1), window(kcol, 2),
            window(vcol, 0), window(vcol, 1), window(vcol, 2),
            pl.BlockSpec((None, 1, NA_HEADS, nq, 3 * nq),
                         lambda b: (layer, jnp.where(b == 0, 0, jnp.where(b == nb - 1, 2, 1)), 0, 0, 0)),
        ],
        out_specs=pl.BlockSpec((nq, width), lambda b: (b, 0)),
        out_shape=jax.ShapeDtypeStruct((S, width), BF16),
        compiler_params=_params("parallel"),
        name="na_attn",
    )(proj, proj, proj, proj, proj, proj, proj, bias)


def _na_bias_tables(rpb, rows):
    br, wr = NA_BLOCK_ROWS, NA_WIN_BLOCK_ROWS
    half = NA_WIN_ROWS // 2
    c = np.arange(GRID_W)
    cs = np.clip(c - NA_WIN_COLS // 2, 0, GRID_W - NA_WIN_COLS)
    col_ok = (c[None, :] >= cs[:, None]) & (c[None, :] < cs[:, None] + NA_WIN_COLS)
    pad = GRID_W - NA_WIN_COLS
    vals = jnp.pad(rpb.astype(F32), ((0, 0), (0, 0), (0, 0), (pad, pad)))
    t_all = jnp.where(col_ok, _toeplitz(vals, GRID_W, GRID_W), NEG_INF)
    L, H = rpb.shape[0], rpb.shape[1]
    types = []
    for b in (0, 1, rows // br - 1):
        win0 = int(np.clip(b * br - half, 0, rows - wr))
        row_blocks = []
        for j in range(br):
            r = b * br + j
            rs = int(np.clip(r - half, 0, rows - NA_WIN_ROWS))
            w_lo, dr_lo = rs - win0, rs - r + (NA_WIN_ROWS - 1)
            piece = jnp.moveaxis(t_all[:, :, dr_lo:dr_lo + NA_WIN_ROWS], 2, 3)
            piece = piece.reshape(L, H, GRID_W, NA_WIN_ROWS * GRID_W)
            row_blocks.append(jnp.pad(
                piece, ((0, 0), (0, 0), (0, 0), (w_lo * GRID_W, (wr - NA_WIN_ROWS - w_lo) * GRID_W)),
                constant_values=NEG_INF))
        types.append(jnp.stack(row_blocks, axis=2).reshape(L, H, br * GRID_W, wr * GRID_W))
    return jnp.stack(types, axis=1)


def _t5_bucket(rel):
    half = T5_BUCKETS // 2
    max_exact = half // 2
    ret = (rel > 0).astype(jnp.int32) * half
    n = jnp.abs(rel)
    nf = jnp.maximum(n, 1).astype(jnp.float32)
    large = max_exact + (jnp.log(nf / max_exact) / math.log(T5_MAX_DIST / max_exact)
                         * (half - max_exact)).astype(jnp.int32)
    large = jnp.minimum(large, half - 1)
    return ret + jnp.where(n < max_exact, n, large)


def _toeplitz(vals, n_rows, n_cols):
    length = n_rows + n_cols - 1
    lead = vals.shape[:-1]
    w = jnp.concatenate([vals, jnp.zeros(lead + (1,), vals.dtype)], axis=-1)
    flat = jnp.tile(w, (1,) * len(lead) + (n_rows,))[..., : n_rows * length]
    return flat.reshape(lead + (n_rows, length))[..., n_rows - 1:]


def _rel_bias(table, lo, hi):
    rel = jnp.arange(lo, hi + 1, dtype=jnp.int32)
    return jnp.transpose(table.astype(F32)[_t5_bucket(rel)], (1, 0))


def _diff_kernel(scal_ref, q_ref, k_ref, v_ref, b_ref, g_ref, lq_ref, o_ref,
                 qa_ref, qb_ref, s0_ref, s1_ref, t0_ref, t1_ref, p0_ref, p1_ref, a0_ref, a1_ref,
                 m_ref, acc_ref, *, nk):
    h = pl.program_id(0)
    qi = pl.program_id(1)
    tq, tk = DIFF_TQ, DIFF_TK
    two_d = 2 * HEAD_DIM

    q = q_ref[0].astype(F32)
    row = lax.broadcasted_iota(jnp.int32, (two_d, tq), 0)
    qa_ref[...] = jnp.where(row < HEAD_DIM, q, 0.0).astype(BF16)
    qb_ref[...] = jnp.where(row >= HEAD_DIM, q, 0.0).astype(BF16)
    m_ref[...] = jnp.full(m_ref.shape, NEG_INF, F32)
    acc_ref[...] = jnp.zeros(acc_ref.shape, F32)
    c_left = scal_ref[h]
    c_right = scal_ref[DIFF_HEADS + h]
    ones_rows = jnp.where(lax.broadcasted_iota(jnp.int32, (DIFF_VROWS - two_d, tk), 0) == 0,
                          1.0, 0.0).astype(BF16)

    chunks = [(c, slice(j, j + DIFF_CHUNK)) for c in range(2) for j in range(0, tq, DIFF_CHUNK)]
    qm_refs = (qa_ref, qb_ref)
    even = (s0_ref, t0_ref, p0_ref, a0_ref)
    odd = (s1_ref, t1_ref, p1_ref, a1_ref)

    def scores(kb, bias_fn, bufs, c, cols):
        s_ref, t_ref = bufs[0], bufs[1]
        s = jnp.dot(kb, qm_refs[c][:, cols], preferred_element_type=F32)
        if bias_fn is not None:
            s = s + bias_fn(cols)
        s_ref[c, :, cols] = s
        t_ref[c, :, cols] = jnp.max(s, axis=0, keepdims=True)

    def softmax(shift, bufs, c, cols):
        s_ref, t_ref, p_ref, a_ref = bufs
        m_old = m_ref[c, :, cols]
        m_new = jnp.maximum(m_old, t_ref[c, :, cols] + shift)
        a_ref[c, :, cols] = jnp.exp2(m_old - m_new)
        p_ref[c, :, cols] = jnp.exp2(s_ref[c, :, cols] - (m_new - shift)).astype(BF16)
        m_ref[c, :, cols] = m_new

    def values(vt, bufs, c, cols):
        p_ref, a_ref = bufs[2], bufs[3]
        acc_ref[c, :, cols] = (a_ref[c, :, cols] * acc_ref[c, :, cols]
                               + jnp.dot(vt, p_ref[c, :, cols], preferred_element_type=F32))

    def block(nxt, cur, prv):
        for c, cols in chunks:
            if nxt is not None:
                kb = k_ref[pl.ds(pl.multiple_of(nxt[0] * tk, tk), tk), :]
                scores(kb, nxt[1], nxt[2], c, cols)
            if cur is not None:
                softmax(cur[0], cur[1], c, cols)
            if prv is not None:
                vt = jnp.concatenate([v_ref[prv[0]], ones_rows], axis=0)
                values(vt, prv[1], c, cols)

    k_left = jnp.maximum(qi - 1, 0)
    k_right = jnp.minimum(qi + 1, nk - 1)
    has_left = qi >= 1
    has_right = qi + 1 < nk

    def near_bias(index, exists):
        def fn(cols):
            bias = b_ref[0, index, :, cols]
            return bias if exists is None else jnp.where(exists, bias, NEG_INF)
        return fn

    n_near = k_right - k_left + 1
    n_slots = nk - 2

    def far(f):
        ki = jnp.minimum(jnp.where(f < k_left, f, f + n_near), nk - 1)
        shift = jnp.where(f < nk - n_near, jnp.where(ki < qi, c_left, c_right), NEG_INF)
        return ki, shift

    far0, shift0 = far(0)
    block((qi, near_bias(1, None), even), None, None)
    block((k_left, near_bias(0, has_left), odd), (0.0, even), None)
    block((k_right, near_bias(2, has_right), even), (0.0, odd), (qi, even))
    block((far0, None, odd), (0.0, even), (k_left, odd))

    def pair(f, k_prev):
        k_a, shift_a = far(f)
        k_b, shift_b = far(f + 1)
        k_c, _ = far(f + 2)
        block((k_b, None, even), (shift_a, odd), (k_prev, even))
        block((k_c, None, odd), (shift_b, even), (k_a, odd))
        return k_b

    def pairs(j, k_prev):
        for i in range(DIFF_PAIRS_PER_TRIP):
            k_prev = pair(2 * (DIFF_PAIRS_PER_TRIP * j + i), k_prev)
        return k_prev

    assert n_slots % (2 * DIFF_PAIRS_PER_TRIP) == 0
    k_last = lax.fori_loop(0, n_slots // (2 * DIFF_PAIRS_PER_TRIP), pairs, k_right)
    block(None, None, (k_last, even))

    lam_init = scal_ref[2 * DIFF_HEADS]
    lq = lq_ref[...]
    lam = (jnp.exp(jnp.sum(lq[0:1] * lq[1:2], axis=-1, keepdims=True))
           - jnp.exp(jnp.sum(lq[2:3] * lq[3:4], axis=-1, keepdims=True)) + lam_init)
    acc0, acc1 = acc_ref[0], acc_ref[1]
    o = (acc0[:two_d] / acc0[two_d:two_d + 1]
         - lam * (acc1[:two_d] / acc1[two_d:two_d + 1]))
    y = o * lax.rsqrt(jnp.mean(o * o, axis=0, keepdims=True) + EPS) * g_ref[...]
    o_ref[...] = jnp.transpose(y * (1.0 - lam_init)).astype(o_ref.dtype)


def _diff(scal, proj, proj_t, bias_t, subln_g, lq):
    S = proj.shape[0]
    tq, tk = DIFF_TQ, DIFF_TK
    nk = S // tk
    assert nk % 2 == 0 and tq == ROW_TILE and tk == ROW_TILE
    two_d = 2 * HEAD_DIM
    kcol = N_KD // LANES
    qrow, vrow = T_QD // two_d, T_VD // two_d
    return pl.pallas_call(
        functools.partial(_diff_kernel, nk=nk),
        grid=(DIFF_HEADS, S // tq),
        in_specs=[
            pl.BlockSpec(memory_space=pltpu.SMEM),
            pl.BlockSpec((1, two_d, tq), lambda h, i: (i, qrow + h, 0)),
            pl.BlockSpec((S, LANES), lambda h, i: (0, kcol + h)),
            pl.BlockSpec((nk, two_d, tk), lambda h, i: (0, vrow + h, 0)),
            pl.BlockSpec((1, 3, tk, tq), lambda h, i: (h, 0, 0, 0)),
            _resident((two_d, 1), lambda h, i: (0, 0)),
            _resident((4, HEAD_DIM), lambda h, i: (0, 0)),
        ],
        out_specs=pl.BlockSpec((tq, two_d), lambda h, i: (i, h)),
        out_shape=jax.ShapeDtypeStruct((S, DIFF_HEADS * two_d), BF16),
        scratch_shapes=[
            pltpu.VMEM((two_d, tq), BF16),
            pltpu.VMEM((two_d, tq), BF16),
            pltpu.VMEM((2, tk, tq), F32),
            pltpu.VMEM((2, tk, tq), F32),
            pltpu.VMEM((2, 1, tq), F32),
            pltpu.VMEM((2, 1, tq), F32),
            pltpu.VMEM((2, tk, tq), BF16),
            pltpu.VMEM((2, tk, tq), BF16),
            pltpu.VMEM((2, 1, tq), F32),
            pltpu.VMEM((2, 1, tq), F32),
            pltpu.VMEM((2, 1, tq), F32),
            pltpu.VMEM((2, DIFF_VROWS, tq), F32),
        ],
        compiler_params=_params("parallel", "arbitrary"),
        name="diff_attn",
    )(scal, proj_t, proj, proj_t, bias_t, subln_g, lq)


def _diff_bias_tiles(table):
    tq, tk = DIFF_TQ, DIFF_TK
    tiles = []
    for d in (-1, 0, 1):
        vals = _rel_bias(table, d * tk - (tq - 1), d * tk + tk - 1)
        tiles.append(jnp.swapaxes(_toeplitz(vals, tq, tk), -1, -2))
    return jnp.stack(tiles, axis=1)


def _gqa_kernel(q_ref, kp_ref, kc_ref, kn_ref, vp_ref, vc_ref, vn_ref, b_ref, sink_ref, o_ref, *, n_blocks):
    blk = pl.program_id(1)
    sub = GQA_QB // WINDOW
    kw = 3 * WINDOW
    width = GQA_GROUP * HEAD_DIM
    qt = q_ref[0].astype(F32)
    kx = jnp.concatenate([kp_ref[...], kc_ref[...], kn_ref[...]], axis=0)
    vt = jnp.concatenate([vp_ref[0], vc_ref[0], vn_ref[0]], axis=1)
    row_head = lax.broadcasted_iota(jnp.int32, (width, WINDOW), 0) // HEAD_DIM
    krow = lax.broadcasted_iota(jnp.int32, (kw, 1), 0)
    ones_rows = jnp.where(lax.broadcasted_iota(jnp.int32, (16, kw), 0) == 0, 1.0, 0.0).astype(BF16)
    sink = sink_ref[0]
    outs = [None] * sub

    def logits(i):
        qi = qt[:, i * WINDOW:(i + 1) * WINDOW]
        qs = jnp.concatenate([jnp.where(row_head == g, qi, 0.0).astype(BF16) for g in range(GQA_GROUP)], axis=1)
        s = jnp.dot(kx[i * WINDOW:i * WINDOW + kw], qs, preferred_element_type=F32) + b_ref[0]
        if i == 0:
            s = jnp.where((krow < WINDOW) & (blk == 0), NEG_INF, s)
        if i == sub - 1:
            s = jnp.where((krow >= 2 * WINDOW) & (blk == n_blocks - 1), NEG_INF, s)
        return s

    def attend(i, s):
        m = jnp.maximum(jnp.max(s, axis=0, keepdims=True), sink)
        p = jnp.exp2(s - m).astype(BF16)
        v_ext = jnp.concatenate([vt[:, i * WINDOW:i * WINDOW + kw], ones_rows], axis=0)
        o = jnp.dot(v_ext, p, preferred_element_type=F32)
        o = o[:HEAD_DIM] / (o[HEAD_DIM:HEAD_DIM + 1] + jnp.exp2(sink - m))
        outs[i] = jnp.concatenate([o[:, g * WINDOW:(g + 1) * WINDOW] for g in range(GQA_GROUP)], axis=0)

    _staggered(sub, logits, attend)
    o_ref[...] = jnp.transpose(jnp.concatenate(outs, axis=1)).astype(o_ref.dtype)


def _gqa(proj, proj_t, bias_t, sink_lanes):
    S = proj.shape[0]
    qb, kw = GQA_QB, 3 * WINDOW
    assert qb == ROW_TILE
    width = GQA_GROUP * HEAD_DIM
    nb = S // qb
    sub = qb // WINDOW
    last = S // WINDOW - 1
    kcol = N_KC // width
    qrow = T_QC // width
    vrow = T_VC // HEAD_DIM

    def prev_blk(b):
        return jnp.maximum(b * sub - 1, 0)

    def next_blk(b):
        return jnp.minimum(b * sub + sub, last)

    return pl.pallas_call(
        functools.partial(_gqa_kernel, n_blocks=nb),
        grid=(GQA_KV_HEADS, nb),
        in_specs=[
            pl.BlockSpec((1, width, qb), lambda kv, b: (b, qrow + kv, 0)),
            pl.BlockSpec((WINDOW, width), lambda kv, b: (prev_blk(b), kcol + kv)),
            pl.BlockSpec((qb, width), lambda kv, b: (b, kcol + kv)),
            pl.BlockSpec((WINDOW, width), lambda kv, b: (next_blk(b), kcol + kv)),
            pl.BlockSpec((1, HEAD_DIM, WINDOW), lambda kv, b: (prev_blk(b) // sub, vrow + kv, prev_blk(b) % sub)),
            pl.BlockSpec((1, HEAD_DIM, qb), lambda kv, b: (b, vrow + kv, 0)),
            pl.BlockSpec((1, HEAD_DIM, WINDOW), lambda kv, b: (next_blk(b) // sub, vrow + kv, next_blk(b) % sub)),
            pl.BlockSpec((1, kw, GQA_GROUP * WINDOW), lambda kv, b: (kv, 0, 0)),
            pl.BlockSpec((1, 1, GQA_GROUP * WINDOW), lambda kv, b: (kv, 0, 0)),
        ],
        out_specs=pl.BlockSpec((qb, width), lambda kv, b: (b, kv)),
        out_shape=jax.ShapeDtypeStruct((S, GQA_Q_HEADS * HEAD_DIM), BF16),
        compiler_params=_params("parallel", "arbitrary"),
        name="gqa_attn",
    )(proj_t, proj, proj, proj, proj_t, proj_t, proj_t, bias_t, sink_lanes)


def _gqa_bias(table):
    qb, kw = WINDOW, 3 * WINDOW
    vals = _rel_bias(table, -WINDOW - (qb - 1), kw - 1 - WINDOW)
    rel = jnp.arange(-WINDOW - (qb - 1), kw - WINDOW, dtype=jnp.int32)
    vals = jnp.where(jnp.abs(rel)[None, :] <= WINDOW, vals, NEG_INF)
    bias = _toeplitz(vals, qb, kw)
    bias = bias.reshape(GQA_KV_HEADS, GQA_GROUP, qb, kw)
    return jnp.transpose(bias, (0, 3, 1, 2)).reshape(GQA_KV_HEADS, kw, GQA_GROUP * qb)


def _merge_kernel(x_ref, ya_ref, yb_ref, yc_ref, g_ref, wg_ref, bg_ref, wb_ref, wo_ref, o_ref):
    x = x_ref[...]
    h = _rms(x, g_ref[...]).astype(BF16)
    gates = _sigmoid(jnp.dot(h, wg_ref[...], preferred_element_type=F32) + bg_ref[...])
    merged = None
    for i, y_ref in enumerate((ya_ref, yb_ref, yc_ref)):
        t = gates[:, i * D_MODEL:(i + 1) * D_MODEL] * jnp.dot(y_ref[...], wb_ref[i], preferred_element_type=F32)
        merged = t if merged is None else merged + t
    o_ref[...] = x + jnp.dot(merged.astype(BF16), wo_ref[...], preferred_element_type=F32)


def _merge(x, ya, yb, yc, g, wg, bg, wb, wo, layer):
    S = x.shape[0]
    tm = ROW_TILE
    branch = pl.BlockSpec((tm, BRANCH_WIDTH), lambda i: (i, 0))
    return pl.pallas_call(
        _merge_kernel,
        grid=(S // tm,),
        in_specs=[
            pl.BlockSpec((tm, D_MODEL), lambda i: (i, 0)),
            branch, branch, branch,
            _resident((1, D_MODEL), lambda i: (0, 0)),
            _resident((None, D_MODEL, N_BRANCHES * D_MODEL), lambda i: (layer, 0, 0)),
            _resident((1, N_BRANCHES * D_MODEL), lambda i: (0, 0)),
            _resident((None, N_BRANCHES, BRANCH_WIDTH, D_MODEL), lambda i: (layer, 0, 0, 0)),
            _resident((None, D_MODEL, D_MODEL), lambda i: (layer, 0, 0)),
        ],
        out_specs=pl.BlockSpec((tm, D_MODEL), lambda i: (i, 0)),
        out_shape=jax.ShapeDtypeStruct((S, D_MODEL), F32),
        compiler_params=_params("parallel"),
        name="merge",
    )(x, ya, yb, yc, g, wg, bg, wb, wo)


def kernel(x, w_in, w_branch, w_gate, b_gate, w_o, norm_g, final_g, ffn_w_gate, ffn_w_up, ffn_w_down,
           na_rpb, diff_lambda, diff_subln_g, gqa_sink, rel_bias_table):
    B, S, _ = x.shape
    assert B == 1 and S % (2 * DIFF_TK) == 0 and S // (NA_BLOCK_ROWS * GRID_W) >= 4
    depth = w_in.shape[0]
    two_d = 2 * HEAD_DIM
    nk = S // DIFF_TK

    col_scale = np.ones((W_IN_COLS,), np.float32)
    for off, width in ((QA_OFF, 512), (QD_OFF, 512), (QC_OFF, 512)):
        col_scale[off:off + width] = HEAD_DIM ** -0.5
    col_scale[QD_OFF:QD_OFF + 512] *= LOG2E
    col_scale[QA_OFF:QA_OFF + 512] *= LOG2E
    col_scale[QC_OFF:QC_OFF + 512] *= LOG2E
    w_in_s = w_in * col_scale
    w_kc_tiled = jnp.tile(w_in_s[:, :, KC_OFF:VC_OFF].reshape(depth, D_MODEL, GQA_KV_HEADS, 1, HEAD_DIM),
                          (1, 1, 1, GQA_GROUP, 1)).reshape(depth, D_MODEL, GQA_Q_HEADS * HEAD_DIM)
    w_in_n = jnp.concatenate([w_in_s[:, :, :QD_OFF], w_in_s[:, :, KD_OFF:VD_OFF], w_kc_tiled], axis=2).astype(BF16)
    w_in_t = jnp.swapaxes(jnp.concatenate(
        [w_in_s[:, :, QD_OFF:KD_OFF], w_in_s[:, :, VD_OFF:QC_OFF], w_in_s[:, :, QC_OFF:KC_OFF],
         w_in_s[:, :, VC_OFF:]], axis=2), 1, 2).astype(BF16)
    w_branch_b = w_branch.astype(BF16)
    w_gate_b = w_gate.astype(BF16)
    w_o_b = w_o.astype(BF16)
    ffn_wg_b = ffn_w_gate.astype(BF16)
    ffn_wu_b = ffn_w_up.astype(BF16)
    ffn_wd_b = ffn_w_down.astype(BF16)

    na_bias = _na_bias_tables(na_rpb * LOG2E, S // GRID_W)
    diff_table = rel_bias_table[:, :DIFF_HEADS]
    diff_bias = _diff_bias_tiles(diff_table) * LOG2E
    far = _rel_bias(diff_table, -T5_MAX_DIST, T5_MAX_DIST) * LOG2E
    gqa_bias = _gqa_bias(rel_bias_table[:, DIFF_HEADS:]) * LOG2E
    final_row = final_g.reshape(1, D_MODEL).astype(F32)

    xs = x.reshape(S, D_MODEL)
    for l in range(depth):
        xs = _ffn(xs, norm_g[l, 0].reshape(1, D_MODEL), ffn_wg_b, ffn_wu_b, ffn_wd_b, final_row, l, 0, False)
        g1 = norm_g[l, 1].reshape(1, D_MODEL)
        proj, proj_t = _proj(xs, g1, w_in_n, w_in_t, l)

        ya = _na(proj, na_bias, l)

        lam_init = 0.8 - 0.6 * math.exp(-0.3 * l)
        scal = jnp.concatenate([far[:, 0], far[:, -1], jnp.full((1,), lam_init, F32)])
        yb = _diff(scal, proj, proj_t, diff_bias, diff_subln_g[l].reshape(two_d, 1).astype(F32),
                   diff_lambda[l].astype(F32))

        sink_lanes = jnp.repeat(gqa_sink[l].astype(F32) * LOG2E, WINDOW).reshape(
            GQA_KV_HEADS, 1, GQA_GROUP * WINDOW)
        yc = _gqa(proj, proj_t, gqa_bias, sink_lanes)

        xs = _merge(xs, ya, yb, yc, g1, w_gate_b, b_gate[l].reshape(1, -1).astype(F32), w_branch_b, w_o_b, l)
        xs = _ffn(xs, norm_g[l, 2].reshape(1, D_MODEL), ffn_wg_b, ffn_wu_b, ffn_wd_b, final_row, l, 1,
                  l == depth - 1)
    return xs.reshape(B, S, D_MODEL)
```

```python
import functools
import math

import jax
import jax.numpy as jnp
import numpy as np
from jax import lax
from jax.experimental import pallas as pl
from jax.experimental.pallas import tpu as pltpu

D_MODEL = 1024
DEPTH = 4
HEAD_DIM = 64
NA_HEADS = 8
DIFF_HEADS = 4
GQA_Q_HEADS = 8
GQA_KV_HEADS = 2
GQA_GROUP = GQA_Q_HEADS // GQA_KV_HEADS
BRANCH_WIDTH = 512
N_BRANCHES = 3
D_FF = 2816
GRID_W = 64
NA_WIN_ROWS = 8
NA_WIN_COLS = 16
WINDOW = 128
T5_BUCKETS = 32
T5_MAX_DIST = 128
NEG_INF = -1e30
EPS = 1e-6

QA_OFF, KA_OFF, VA_OFF = 0, 512, 1024
QD_OFF, KD_OFF, VD_OFF = 1536, 2048, 2560
QC_OFF, KC_OFF, VC_OFF = 3072, 3584, 3712
W_IN_COLS = 3840
N_QA, N_KA, N_VA, N_KD, N_KC = 0, 512, 1024, 1536, 2048
PROJ_N_COLS = 2560
T_QD, T_VD, T_QC, T_VC = 0, 512, 1024, 1536
PROJ_T_ROWS = 1664

LANES = 128
VMEM_LIMIT = 56 * 1024 * 1024

ROW_TILE = 512
FFN_ROW_TILE = 512
NA_BLOCK_ROWS = 4
NA_WIN_BLOCK_ROWS = 12
DIFF_TQ = 512
DIFF_TK = 512
DIFF_CHUNK = 256
DIFF_PAIRS_PER_TRIP = 5
DIFF_VROWS = 2 * HEAD_DIM + 16
LOG2E = math.log2(math.e)
GQA_QB = 512

F32 = jnp.float32
BF16 = jnp.bfloat16


def _resident(shape, index_map):
    return pl.BlockSpec(shape, index_map, pipeline_mode=pl.Buffered(1))


def _params(*sem, flags=None):
    return pltpu.CompilerParams(dimension_semantics=sem, vmem_limit_bytes=VMEM_LIMIT, flags=flags)


def _rms(x, g):
    return x * lax.rsqrt(jnp.mean(x * x, axis=-1, keepdims=True) + EPS) * g


def _sigmoid(z):
    return 1.0 / (1.0 + jnp.exp(-z))


def _ffn_kernel(x_ref, g_ref, wg_ref, wu_ref, wd_ref, fg_ref, o_ref, *, final):
    x = x_ref[...]
    h = _rms(x, g_ref[...]).astype(BF16)
    gate = jnp.dot(h, wg_ref[...], preferred_element_type=F32)
    up = jnp.dot(h, wu_ref[...], preferred_element_type=F32)
    a = (gate * _sigmoid(gate) * up).astype(BF16)
    y = x + 0.5 * jnp.dot(a, wd_ref[...], preferred_element_type=F32)
    if final:
        y = _rms(y, fg_ref[...])
    o_ref[...] = y


def _ffn(x, g, wg, wu, wd, fg, layer, which, final):
    S = x.shape[0]
    tm = FFN_ROW_TILE
    pick = lambda i: (layer, which, 0, 0)
    return pl.pallas_call(
        functools.partial(_ffn_kernel, final=final),
        grid=(S // tm,),
        in_specs=[
            pl.BlockSpec((tm, D_MODEL), lambda i: (i, 0)),
            _resident((1, D_MODEL), lambda i: (0, 0)),
            _resident((None, None, D_MODEL, D_FF), pick),
            _resident((None, None, D_MODEL, D_FF), pick),
            _resident((None, None, D_FF, D_MODEL), pick),
            _resident((1, D_MODEL), lambda i: (0, 0)),
        ],
        out_specs=pl.BlockSpec((tm, D_MODEL), lambda i: (i, 0)),
        out_shape=jax.ShapeDtypeStruct((S, D_MODEL), F32),
        compiler_params=_params("parallel"),
        name="ffn",
    )(x, g, wg, wu, wd, fg)


def _proj_kernel(x_ref, g_ref, wn_ref, wt_ref, on_ref, ot_ref):
    h = _rms(x_ref[...], g_ref[...]).astype(BF16)
    on_ref[...] = jnp.dot(h, wn_ref[...], preferred_element_type=F32).astype(BF16)
    ot_ref[0] = lax.dot_general(wt_ref[...], h, (((1,), (1,)), ((), ())),
                                preferred_element_type=F32).astype(BF16)


def _proj(x, g, wn, wt, layer):
    S = x.shape[0]
    tm = ROW_TILE
    return pl.pallas_call(
        _proj_kernel,
        grid=(S // tm,),
        in_specs=[
            pl.BlockSpec((tm, D_MODEL), lambda i: (i, 0)),
            _resident((1, D_MODEL), lambda i: (0, 0)),
            _resident((None, D_MODEL, PROJ_N_COLS), lambda i: (layer, 0, 0)),
            _resident((None, PROJ_T_ROWS, D_MODEL), lambda i: (layer, 0, 0)),
        ],
        out_specs=[
            pl.BlockSpec((tm, PROJ_N_COLS), lambda i: (i, 0)),
            pl.BlockSpec((1, PROJ_T_ROWS, tm), lambda i: (i, 0, 0)),
        ],
        out_shape=[
            jax.ShapeDtypeStruct((S, PROJ_N_COLS), BF16),
            jax.ShapeDtypeStruct((S // tm, PROJ_T_ROWS, tm), BF16),
        ],
        compiler_params=_params("parallel"),
        name="proj",
    )(x, g, wn, wt)


def _staggered(n, stage_a, stage_b):
    cur = stage_a(0)
    for i in range(n):
        nxt = stage_a(i + 1) if i + 1 < n else None
        stage_b(i, cur)
        cur = nxt


def _na_kernel(q_ref, k0_ref, k1_ref, k2_ref, v0_ref, v1_ref, v2_ref, b_ref, o_ref):
    nq = NA_BLOCK_ROWS * GRID_W
    q = q_ref[...].astype(F32)
    k3 = jnp.concatenate([k0_ref[...], k1_ref[...], k2_ref[...]], axis=0)
    v3 = jnp.concatenate([v0_ref[...], v1_ref[...], v2_ref[...]], axis=0)
    first = lax.broadcasted_iota(jnp.int32, (nq, LANES), 1) < HEAD_DIM
    outs = [None] * NA_HEADS

    def logits(head):
        cols = slice((head // 2) * LANES, (head // 2 + 1) * LANES)
        keep = first if head % 2 == 0 else jnp.logical_not(first)
        qm = jnp.where(keep, q[:, cols], 0.0).astype(BF16)
        s = lax.dot_general(qm, k3[:, cols], (((1,), (1,)), ((), ())), preferred_element_type=F32)
        return s + b_ref[0, head]

    def attend(head, s):
        cols = slice((head // 2) * LANES, (head // 2 + 1) * LANES)
        m = jnp.max(s, axis=-1, keepdims=True)
        p = jnp.exp2(s - m)
        l = jnp.sum(p, axis=-1, keepdims=True)
        outs[head] = jnp.dot(p.astype(BF16), v3[:, cols], preferred_element_type=F32) / l

    _staggered(NA_HEADS, logits, attend)
    for pair in range(NA_HEADS // 2):
        o_ref[:, pair * LANES:(pair + 1) * LANES] = jnp.where(
            first, outs[2 * pair], outs[2 * pair + 1]).astype(o_ref.dtype)


def _na(proj, bias, layer):
    S = proj.shape[0]
    nq = NA_BLOCK_ROWS * GRID_W
    nb = S // nq
    width = NA_HEADS * HEAD_DIM
    qcol, kcol, vcol = N_QA // width, N_KA // width, N_VA // width

    def window(col, j):
        return pl.BlockSpec((nq, width), lambda b: (jnp.clip(b - 1, 0, nb - 3) + j, col))

    return pl.pallas_call(
        _na_kernel,
        grid=(nb,),
        in_specs=[
            pl.BlockSpec((nq, width), lambda b: (b, qcol)),
            window(kcol, 0), window(kcol, 1), window(kcol, 2),
            window(vcol, 0), window(vcol, 1), window(vcol, 2),
            pl.BlockSpec((None, 1, NA_HEADS, nq, 3 * nq),
                         lambda b: (layer, jnp.where(b == 0, 0, jnp.where(b == nb - 1, 2, 1)), 0, 0, 0)),
        ],
        out_specs=pl.BlockSpec((nq, width), lambda b: (b, 0)),
        out_shape=jax.ShapeDtypeStruct((S, width), BF16),
        compiler_params=_params("parallel"),
        name="na_attn",
    )(proj, proj, proj, proj, proj, proj, proj, bias)


def _na_bias_tables(rpb, rows):
    br, wr = NA_BLOCK_ROWS, NA_WIN_BLOCK_ROWS
    half = NA_WIN_ROWS // 2
    c = np.arange(GRID_W)
    cs = np.clip(c - NA_WIN_COLS // 2, 0, GRID_W - NA_WIN_COLS)
    col_ok = (c[None, :] >= cs[:, None]) & (c[None, :] < cs[:, None] + NA_WIN_COLS)
    pad = GRID_W - NA_WIN_COLS
    vals = jnp.pad(rpb.astype(F32), ((0, 0), (0, 0), (0, 0), (pad, pad)))
    t_all = jnp.where(col_ok, _toeplitz(vals, GRID_W, GRID_W), NEG_INF)
    L, H = rpb.shape[0], rpb.shape[1]
    n_dr = 2 * NA_WIN_ROWS - 1
    seq = jnp.moveaxis(t_all, 2, 3).reshape(L, H, GRID_W, n_dr * GRID_W)
    seq = jnp.pad(seq, ((0, 0), (0, 0), (0, 0), (wr * GRID_W, wr * GRID_W)), constant_values=NEG_INF)
    dr_of_lane = np.arange((n_dr + 2 * wr) * GRID_W) // GRID_W - wr
    types = []
    for b in (0, 1, rows // br - 1):
        win0 = int(np.clip(b * br - half, 0, rows - wr))
        row_blocks = []
        for j in range(br):
            r = b * br + j
            rs = int(np.clip(r - half, 0, rows - NA_WIN_ROWS))
            w_lo, dr_lo = rs - win0, rs - r + (NA_WIN_ROWS - 1)
            inside = (dr_of_lane >= dr_lo) & (dr_of_lane < dr_lo + NA_WIN_ROWS)
            start = (wr + dr_lo - w_lo) * GRID_W
            row_blocks.append(jnp.where(inside, seq, NEG_INF)[..., start:start + wr * GRID_W])
        types.append(jnp.concatenate(row_blocks, axis=2))
    return jnp.stack(types, axis=1)


def _t5_bucket(rel):
    half = T5_BUCKETS // 2
    max_exact = half // 2
    ret = (rel > 0).astype(jnp.int32) * half
    n = jnp.abs(rel)
    nf = jnp.maximum(n, 1).astype(jnp.float32)
    large = max_exact + (jnp.log(nf / max_exact) / math.log(T5_MAX_DIST / max_exact)
                         * (half - max_exact)).astype(jnp.int32)
    large = jnp.minimum(large, half - 1)
    return ret + jnp.where(n < max_exact, n, large)


def _toeplitz(vals, n_rows, n_cols):
    length = n_rows + n_cols - 1
    lead = vals.shape[:-1]
    w = jnp.concatenate([vals, jnp.zeros(lead + (1,), vals.dtype)], axis=-1)
    flat = jnp.tile(w, (1,) * len(lead) + (n_rows,))[..., : n_rows * length]
    return flat.reshape(lead + (n_rows, length))[..., n_rows - 1:]


def _rel_bias(table, lo, hi):
    rel = jnp.arange(lo, hi + 1, dtype=jnp.int32)
    return jnp.transpose(table.astype(F32)[_t5_bucket(rel)], (1, 0))


def _diff_kernel(scal_ref, q_ref, k_ref, v_ref, b_ref, g_ref, lq_ref, o_ref,
                 qa_ref, qb_ref, s0_ref, s1_ref, t0_ref, t1_ref, p0_ref, p1_ref, a0_ref, a1_ref,
                 m_ref, acc_ref, *, nk):
    h = pl.program_id(0)
    qi = pl.program_id(1)
    tq, tk = DIFF_TQ, DIFF_TK
    two_d = 2 * HEAD_DIM

    q = q_ref[0].astype(F32)
    row = lax.broadcasted_iota(jnp.int32, (two_d, tq), 0)
    qa_ref[...] = jnp.where(row < HEAD_DIM, q, 0.0).astype(BF16)
    qb_ref[...] = jnp.where(row >= HEAD_DIM, q, 0.0).astype(BF16)
    m_ref[...] = jnp.full(m_ref.shape, NEG_INF, F32)
    acc_ref[...] = jnp.zeros(acc_ref.shape, F32)
    c_left = scal_ref[h]
    c_right = scal_ref[DIFF_HEADS + h]
    ones_rows = jnp.where(lax.broadcasted_iota(jnp.int32, (DIFF_VROWS - two_d, tk), 0) == 0,
                          1.0, 0.0).astype(BF16)

    chunks = [(c, slice(j, j + DIFF_CHUNK)) for c in range(2) for j in range(0, tq, DIFF_CHUNK)]
    qm_refs = (qa_ref, qb_ref)
    even = (s0_ref, t0_ref, p0_ref, a0_ref)
    odd = (s1_ref, t1_ref, p1_ref, a1_ref)

    def scores(kb, bias_fn, bufs, c, cols):
        s_ref, t_ref = bufs[0], bufs[1]
        s = jnp.dot(kb, qm_refs[c][:, cols], preferred_element_type=F32)
        if bias_fn is not None:
            s = s + bias_fn(cols)
        s_ref[c, :, cols] = s
        t_ref[c, :, cols] = jnp.max(s, axis=0, keepdims=True)

    def softmax(shift, bufs, c, cols):
        s_ref, t_ref, p_ref, a_ref = bufs
        m_old = m_ref[c, :, cols]
        m_new = jnp.maximum(m_old, t_ref[c, :, cols] + shift)
        a_ref[c, :, cols] = jnp.exp2(m_old - m_new)
        p_ref[c, :, cols] = jnp.exp2(s_ref[c, :, cols] - (m_new - shift)).astype(BF16)
        m_ref[c, :, cols] = m_new

    def values(vt, bufs, c, cols):
        p_ref, a_ref = bufs[2], bufs[3]
        acc_ref[c, :, cols] = (a_ref[c, :, cols] * acc_ref[c, :, cols]
                               + jnp.dot(vt, p_ref[c, :, cols], preferred_element_type=F32))

    def block(nxt, cur, prv):
        for c, cols in chunks:
            if nxt is not None:
                kb = k_ref[pl.ds(pl.multiple_of(nxt[0] * tk, tk), tk), :]
                scores(kb, nxt[1], nxt[2], c, cols)
            if cur is not None:
                softmax(cur[0], cur[1], c, cols)
            if prv is not None:
                vt = jnp.concatenate([v_ref[prv[0]], ones_rows], axis=0)
                values(vt, prv[1], c, cols)

    k_left = jnp.maximum(qi - 1, 0)
    k_right = jnp.minimum(qi + 1, nk - 1)
    has_left = qi >= 1
    has_right = qi + 1 < nk

    def near_bias(index, exists):
        def fn(cols):
            bias = b_ref[0, index, :, cols]
            return bias if exists is None else jnp.where(exists, bias, NEG_INF)
        return fn

    n_near = k_right - k_left + 1
    n_slots = nk - 2

    def far(f):
        ki = jnp.minimum(jnp.where(f < k_left, f, f + n_near), nk - 1)
        shift = jnp.where(f < nk - n_near, jnp.where(ki < qi, c_left, c_right), NEG_INF)
        return ki, shift

    far0, shift0 = far(0)
    block((qi, near_bias(1, None), even), None, None)
    block((k_left, near_bias(0, has_left), odd), (0.0, even), None)
    block((k_right, near_bias(2, has_right), even), (0.0, odd), (qi, even))
    block((far0, None, odd), (0.0, even), (k_left, odd))

    def pair(f, k_prev):
        k_a, shift_a = far(f)
        k_b, shift_b = far(f + 1)
        k_c, _ = far(f + 2)
        block((k_b, None, even), (shift_a, odd), (k_prev, even))
        block((k_c, None, odd), (shift_b, even), (k_a, odd))
        return k_b

    def pairs(j, k_prev):
        for i in range(DIFF_PAIRS_PER_TRIP):
            k_prev = pair(2 * (DIFF_PAIRS_PER_TRIP * j + i), k_prev)
        return k_prev

    assert n_slots % (2 * DIFF_PAIRS_PER_TRIP) == 0
    k_last = lax.fori_loop(0, n_slots // (2 * DIFF_PAIRS_PER_TRIP), pairs, k_right)
    block(None, None, (k_last, even))

    lam_init = scal_ref[2 * DIFF_HEADS]
    lq = lq_ref[...]
    lam = (jnp.exp(jnp.sum(lq[0:1] * lq[1:2], axis=-1, keepdims=True))
           - jnp.exp(jnp.sum(lq[2:3] * lq[3:4], axis=-1, keepdims=True)) + lam_init)
    acc0, acc1 = acc_ref[0], acc_ref[1]
    o = (acc0[:two_d] / acc0[two_d:two_d + 1]
         - lam * (acc1[:two_d] / acc1[two_d:two_d + 1]))
    y = o * lax.rsqrt(jnp.mean(o * o, axis=0, keepdims=True) + EPS) * g_ref[...]
    o_ref[...] = jnp.transpose(y * (1.0 - lam_init)).astype(o_ref.dtype)


def _diff(scal, proj, proj_t, bias_t, subln_g, lq):
    S = proj.shape[0]
    tq, tk = DIFF_TQ, DIFF_TK
    nk = S // tk
    assert nk % 2 == 0 and tq == ROW_TILE and tk == ROW_TILE
    two_d = 2 * HEAD_DIM
    kcol = N_KD // LANES
    qrow, vrow = T_QD // two_d, T_VD // two_d
    return pl.pallas_call(
        functools.partial(_diff_kernel, nk=nk),
        grid=(DIFF_HEADS, S // tq),
        in_specs=[
            pl.BlockSpec(memory_space=pltpu.SMEM),
            pl.BlockSpec((1, two_d, tq), lambda h, i: (i, qrow + h, 0)),
            pl.BlockSpec((S, LANES), lambda h, i: (0, kcol + h)),
            pl.BlockSpec((nk, two_d, tk), lambda h, i: (0, vrow + h, 0)),
            pl.BlockSpec((1, 3, tk, tq), lambda h, i: (h, 0, 0, 0)),
            _resident((two_d, 1), lambda h, i: (0, 0)),
            _resident((4, HEAD_DIM), lambda h, i: (0, 0)),
        ],
        out_specs=pl.BlockSpec((tq, two_d), lambda h, i: (i, h)),
        out_shape=jax.ShapeDtypeStruct((S, DIFF_HEADS * two_d), BF16),
        scratch_shapes=[
            pltpu.VMEM((two_d, tq), BF16),
            pltpu.VMEM((two_d, tq), BF16),
            pltpu.VMEM((2, tk, tq), F32),
            pltpu.VMEM((2, tk, tq), F32),
            pltpu.VMEM((2, 1, tq), F32),
            pltpu.VMEM((2, 1, tq), F32),
            pltpu.VMEM((2, tk, tq), BF16),
            pltpu.VMEM((2, tk, tq), BF16),
            pltpu.VMEM((2, 1, tq), F32),
            pltpu.VMEM((2, 1, tq), F32),
            pltpu.VMEM((2, 1, tq), F32),
            pltpu.VMEM((2, DIFF_VROWS, tq), F32),
        ],
        compiler_params=_params("parallel", "arbitrary"),
        name="diff_attn",
    )(scal, proj_t, proj, proj_t, bias_t, subln_g, lq)


def _diff_bias_tiles(table):
    tq, tk = DIFF_TQ, DIFF_TK
    tiles = []
    for d in (-1, 0, 1):
        vals = _rel_bias(table, d * tk - (tq - 1), d * tk + tk - 1)
        tiles.append(jnp.swapaxes(_toeplitz(vals, tq, tk), -1, -2))
    return jnp.stack(tiles, axis=1)


def _gqa_kernel(q_ref, kp_ref, kc_ref, kn_ref, vp_ref, vc_ref, vn_ref, b_ref, sink_ref, o_ref, *, n_blocks):
    blk = pl.program_id(1)
    sub = GQA_QB // WINDOW
    kw = 3 * WINDOW
    width = GQA_GROUP * HEAD_DIM
    qt = q_ref[0].astype(F32)
    kx = jnp.concatenate([kp_ref[...], kc_ref[...], kn_ref[...]], axis=0)
    vt = jnp.concatenate([vp_ref[0], vc_ref[0], vn_ref[0]], axis=1)
    row_head = lax.broadcasted_iota(jnp.int32, (width, WINDOW), 0) // HEAD_DIM
    krow = lax.broadcasted_iota(jnp.int32, (kw, 1), 0)
    ones_rows = jnp.where(lax.broadcasted_iota(jnp.int32, (16, kw), 0) == 0, 1.0, 0.0).astype(BF16)
    sink = sink_ref[0]
    outs = [None] * sub

    def logits(i):
        qi = qt[:, i * WINDOW:(i + 1) * WINDOW]
        qs = jnp.concatenate([jnp.where(row_head == g, qi, 0.0).astype(BF16) for g in range(GQA_GROUP)], axis=1)
        s = jnp.dot(kx[i * WINDOW:i * WINDOW + kw], qs, preferred_element_type=F32) + b_ref[0]
        if i == 0:
            s = jnp.where((krow < WINDOW) & (blk == 0), NEG_INF, s)
        if i == sub - 1:
            s = jnp.where((krow >= 2 * WINDOW) & (blk == n_blocks - 1), NEG_INF, s)
        return s

    def attend(i, s):
        m = jnp.maximum(jnp.max(s, axis=0, keepdims=True), sink)
        p = jnp.exp2(s - m).astype(BF16)
        v_ext = jnp.concatenate([vt[:, i * WINDOW:i * WINDOW + kw], ones_rows], axis=0)
        o = jnp.dot(v_ext, p, preferred_element_type=F32)
        o = o[:HEAD_DIM] / (o[HEAD_DIM:HEAD_DIM + 1] + jnp.exp2(sink - m))
        outs[i] = jnp.concatenate([o[:, g * WINDOW:(g + 1) * WINDOW] for g in range(GQA_GROUP)], axis=0)

    _staggered(sub, logits, attend)
    o_ref[...] = jnp.transpose(jnp.concatenate(outs, axis=1)).astype(o_ref.dtype)


def _gqa(proj, proj_t, bias_t, sink_lanes):
    S = proj.shape[0]
    qb, kw = GQA_QB, 3 * WINDOW
    assert qb == ROW_TILE
    width = GQA_GROUP * HEAD_DIM
    nb = S // qb
    sub = qb // WINDOW
    last = S // WINDOW - 1
    kcol = N_KC // width
    qrow = T_QC // width
    vrow = T_VC // HEAD_DIM

    def prev_blk(b):
        return jnp.maximum(b * sub - 1, 0)

    def next_blk(b):
        return jnp.minimum(b * sub + sub, last)

    return pl.pallas_call(
        functools.partial(_gqa_kernel, n_blocks=nb),
        grid=(GQA_KV_HEADS, nb),
        in_specs=[
            pl.BlockSpec((1, width, qb), lambda kv, b: (b, qrow + kv, 0)),
            pl.BlockSpec((WINDOW, width), lambda kv, b: (prev_blk(b), kcol + kv)),
            pl.BlockSpec((qb, width), lambda kv, b: (b, kcol + kv)),
            pl.BlockSpec((WINDOW, width), lambda kv, b: (next_blk(b), kcol + kv)),
            pl.BlockSpec((1, HEAD_DIM, WINDOW), lambda kv, b: (prev_blk(b) // sub, vrow + kv, prev_blk(b) % sub)),
            pl.BlockSpec((1, HEAD_DIM, qb), lambda kv, b: (b, vrow + kv, 0)),
            pl.BlockSpec((1, HEAD_DIM, WINDOW), lambda kv, b: (next_blk(b) // sub, vrow + kv, next_blk(b) % sub)),
            pl.BlockSpec((1, kw, GQA_GROUP * WINDOW), lambda kv, b: (kv, 0, 0)),
            pl.BlockSpec((1, 1, GQA_GROUP * WINDOW), lambda kv, b: (kv, 0, 0)),
        ],
        out_specs=pl.BlockSpec((qb, width), lambda kv, b: (b, kv)),
        out_shape=jax.ShapeDtypeStruct((S, GQA_Q_HEADS * HEAD_DIM), BF16),
        compiler_params=_params("parallel", "arbitrary"),
        name="gqa_attn",
    )(proj_t, proj, proj, proj, proj_t, proj_t, proj_t, bias_t, sink_lanes)


def _gqa_bias(table):
    qb, kw = WINDOW, 3 * WINDOW
    vals = _rel_bias(table, -WINDOW - (qb - 1), kw - 1 - WINDOW)
    rel = jnp.arange(-WINDOW - (qb - 1), kw - WINDOW, dtype=jnp.int32)
    vals = jnp.where(jnp.abs(rel)[None, :] <= WINDOW, vals, NEG_INF)
    bias = _toeplitz(vals, qb, kw)
    bias = bias.reshape(GQA_KV_HEADS, GQA_GROUP, qb, kw)
    return jnp.transpose(bias, (0, 3, 1, 2)).reshape(GQA_KV_HEADS, kw, GQA_GROUP * qb)


def _merge_kernel(x_ref, ya_ref, yb_ref, yc_ref, g_ref, wg_ref, bg_ref, wb_ref, wo_ref, o_ref):
    x = x_ref[...]
    h = _rms(x, g_ref[...]).astype(BF16)
    gates = _sigmoid(jnp.dot(h, wg_ref[...], preferred_element_type=F32) + bg_ref[...])
    merged = None
    for i, y_ref in enumerate((ya_ref, yb_ref, yc_ref)):
        t = gates[:, i * D_MODEL:(i + 1) * D_MODEL] * jnp.dot(y_ref[...], wb_ref[i], preferred_element_type=F32)
        merged = t if merged is None else merged + t
    o_ref[...] = x + jnp.dot(merged.astype(BF16), wo_ref[...], preferred_element_type=F32)


def _merge(x, ya, yb, yc, g, wg, bg, wb, wo, layer):
    S = x.shape[0]
    tm = ROW_TILE
    branch = pl.BlockSpec((tm, BRANCH_WIDTH), lambda i: (i, 0))
    return pl.pallas_call(
        _merge_kernel,
        grid=(S // tm,),
        in_specs=[
            pl.BlockSpec((tm, D_MODEL), lambda i: (i, 0)),
            branch, branch, branch,
            _resident((1, D_MODEL), lambda i: (0, 0)),
            _resident((None, D_MODEL, N_BRANCHES * D_MODEL), lambda i: (layer, 0, 0)),
            _resident((1, N_BRANCHES * D_MODEL), lambda i: (0, 0)),
            _resident((None, N_BRANCHES, BRANCH_WIDTH, D_MODEL), lambda i: (layer, 0, 0, 0)),
            _resident((None, D_MODEL, D_MODEL), lambda i: (layer, 0, 0)),
        ],
        out_specs=pl.BlockSpec((tm, D_MODEL), lambda i: (i, 0)),
        out_shape=jax.ShapeDtypeStruct((S, D_MODEL), F32),
        compiler_params=_params("parallel"),
        name="merge",
    )(x, ya, yb, yc, g, wg, bg, wb, wo)


def kernel(x, w_in, w_branch, w_gate, b_gate, w_o, norm_g, final_g, ffn_w_gate, ffn_w_up, ffn_w_down,
           na_rpb, diff_lambda, diff_subln_g, gqa_sink, rel_bias_table):
    B, S, _ = x.shape
    assert B == 1 and S % (2 * DIFF_TK) == 0 and S // (NA_BLOCK_ROWS * GRID_W) >= 4
    depth = w_in.shape[0]
    two_d = 2 * HEAD_DIM
    nk = S // DIFF_TK

    col_scale = np.ones((W_IN_COLS,), np.float32)
    for off, width in ((QA_OFF, 512), (QD_OFF, 512), (QC_OFF, 512)):
        col_scale[off:off + width] = HEAD_DIM ** -0.5
    col_scale[QD_OFF:QD_OFF + 512] *= LOG2E
    col_scale[QA_OFF:QA_OFF + 512] *= LOG2E
    col_scale[QC_OFF:QC_OFF + 512] *= LOG2E
    w_in_s = (w_in * col_scale).astype(BF16)
    w_kc_tiled = jnp.tile(w_in_s[:, :, KC_OFF:VC_OFF].reshape(depth, D_MODEL, GQA_KV_HEADS, 1, HEAD_DIM),
                          (1, 1, 1, GQA_GROUP, 1)).reshape(depth, D_MODEL, GQA_Q_HEADS * HEAD_DIM)
    w_in_n = jnp.concatenate([w_in_s[:, :, :QD_OFF], w_in_s[:, :, KD_OFF:VD_OFF], w_kc_tiled], axis=2)
    w_in_t = jnp.swapaxes(jnp.concatenate(
        [w_in_s[:, :, QD_OFF:KD_OFF], w_in_s[:, :, VD_OFF:QC_OFF], w_in_s[:, :, QC_OFF:KC_OFF],
         w_in_s[:, :, VC_OFF:]], axis=2), 1, 2)
    w_branch_b = w_branch.astype(BF16)
    w_gate_b = w_gate.astype(BF16)
    w_o_b = w_o.astype(BF16)
    ffn_wg_b = ffn_w_gate.astype(BF16)
    ffn_wu_b = ffn_w_up.astype(BF16)
    ffn_wd_b = ffn_w_down.astype(BF16)

    na_bias = _na_bias_tables(na_rpb * LOG2E, S // GRID_W)
    diff_table = rel_bias_table[:, :DIFF_HEADS]
    diff_bias = _diff_bias_tiles(diff_table) * LOG2E
    far = _rel_bias(diff_table, -T5_MAX_DIST, T5_MAX_DIST) * LOG2E
    gqa_bias = _gqa_bias(rel_bias_table[:, DIFF_HEADS:]) * LOG2E
    final_row = final_g.reshape(1, D_MODEL).astype(F32)

    xs = x.reshape(S, D_MODEL)
    for l in range(depth):
        xs = _ffn(xs, norm_g[l, 0].reshape(1, D_MODEL), ffn_wg_b, ffn_wu_b, ffn_wd_b, final_row, l, 0, False)
        g1 = norm_g[l, 1].reshape(1, D_MODEL)
        proj, proj_t = _proj(xs, g1, w_in_n, w_in_t, l)

        ya = _na(proj, na_bias, l)

        lam_init = 0.8 - 0.6 * math.exp(-0.3 * l)
        scal = jnp.concatenate([far[:, 0], far[:, -1], jnp.full((1,), lam_init, F32)])
        yb = _diff(scal, proj, proj_t, diff_bias, diff_subln_g[l].reshape(two_d, 1).astype(F32),
                   diff_lambda[l].astype(F32))

        sink_lanes = jnp.repeat(gqa_sink[l].astype(F32) * LOG2E, WINDOW).reshape(
            GQA_KV_HEADS, 1, GQA_GROUP * WINDOW)
        yc = _gqa(proj, proj_t, gqa_bias, sink_lanes)

        xs = _merge(xs, ya, yb, yc, g1, w_gate_b, b_gate[l].reshape(1, -1).astype(F32), w_branch_b, w_o_b, l)
        xs = _ffn(xs, norm_g[l, 2].reshape(1, D_MODEL), ffn_wg_b, ffn_wu_b, ffn_wd_b, final_row, l, 1,
                  l == depth - 1)
    return xs.reshape(B, S, D_MODEL)
```

```python
import functools
import math

import jax
import jax.numpy as jnp
import numpy as np
from jax import lax
from jax.experimental import pallas as pl
from jax.experimental.pallas import tpu as pltpu

D_MODEL = 1024
HEAD_DIM = 64
NA_HEADS = 8
DIFF_HEADS = 4
GQA_Q_HEADS = 8
GQA_KV_HEADS = 2
GQA_GROUP = GQA_Q_HEADS // GQA_KV_HEADS
BRANCH_WIDTH = 512
N_BRANCHES = 3
D_FF = 2816
GRID_W = 64
NA_WIN_ROWS = 8
NA_WIN_COLS = 16
WINDOW = 128
T5_BUCKETS = 32
T5_MAX_DIST = 128
NEG_INF = -1e30
EPS = 1e-6

QA_OFF, KA_OFF, VA_OFF = 0, 512, 1024
QD_OFF, KD_OFF, VD_OFF = 1536, 2048, 2560
QC_OFF, KC_OFF, VC_OFF = 3072, 3584, 3712
W_IN_COLS = 3840
N_QA, N_KA, N_VA, N_KD, N_KC = 0, 512, 1024, 1536, 2048
PROJ_N_COLS = 2560
T_QD, T_VD, T_QC, T_VC = 0, 512, 1024, 1536
PROJ_T_ROWS = 1664

LANES = 128
BF16_ROWS = 16
VMEM_LIMIT = 56 * 1024 * 1024

ROW_TILE = 512
NA_BLOCK_ROWS = 4
NA_WIN_BLOCK_ROWS = 12
DIFF_TQ = 512
DIFF_TK = 512
DIFF_CHUNK = 256
DIFF_MAX_PAIRS_PER_TRIP = 7
DIFF_VROWS = 2 * HEAD_DIM + BF16_ROWS
LOG2E = math.log2(math.e)
GQA_QB = 512

F32 = jnp.float32
BF16 = jnp.bfloat16


def _resident(shape, index_map):
    return pl.BlockSpec(shape, index_map, pipeline_mode=pl.Buffered(1))


def _params(*sem):
    return pltpu.CompilerParams(dimension_semantics=sem, vmem_limit_bytes=VMEM_LIMIT)


def _rms(x, g):
    return x * lax.rsqrt(jnp.mean(x * x, axis=-1, keepdims=True) + EPS) * g


def _sigmoid(z):
    return 1.0 / (1.0 + jnp.exp(-z))


def _ffn_kernel(x_ref, g_ref, wg_ref, wu_ref, wd_ref, fg_ref, o_ref, *, final):
    x = x_ref[...]
    h = _rms(x, g_ref[...]).astype(BF16)
    gate = jnp.dot(h, wg_ref[...], preferred_element_type=F32)
    up = jnp.dot(h, wu_ref[...], preferred_element_type=F32)
    a = (gate * _sigmoid(gate) * up).astype(BF16)
    y = x + 0.5 * jnp.dot(a, wd_ref[...], preferred_element_type=F32)
    if final:
        y = _rms(y, fg_ref[...])
    o_ref[...] = y


def _ffn(x, g, wg, wu, wd, fg, layer, which, final):
    S = x.shape[0]
    tm = ROW_TILE
    pick = lambda i: (layer, which, 0, 0)
    return pl.pallas_call(
        functools.partial(_ffn_kernel, final=final),
        grid=(S // tm,),
        in_specs=[
            pl.BlockSpec((tm, D_MODEL), lambda i: (i, 0)),
            _resident((1, D_MODEL), lambda i: (0, 0)),
            _resident((None, None, D_MODEL, D_FF), pick),
            _resident((None, None, D_MODEL, D_FF), pick),
            _resident((None, None, D_FF, D_MODEL), pick),
            _resident((1, D_MODEL), lambda i: (0, 0)),
        ],
        out_specs=pl.BlockSpec((tm, D_MODEL), lambda i: (i, 0)),
        out_shape=jax.ShapeDtypeStruct((S, D_MODEL), F32),
        compiler_params=_params("parallel"),
        name="ffn",
    )(x, g, wg, wu, wd, fg)


def _proj_kernel(x_ref, g_ref, wn_ref, wt_ref, on_ref, ot_ref):
    h = _rms(x_ref[...], g_ref[...]).astype(BF16)
    on_ref[...] = jnp.dot(h, wn_ref[...], preferred_element_type=F32).astype(BF16)
    ot_ref[0] = lax.dot_general(wt_ref[...], h, (((1,), (1,)), ((), ())),
                                preferred_element_type=F32).astype(BF16)


def _proj(x, g, wn, wt, layer):
    S = x.shape[0]
    tm = ROW_TILE
    return pl.pallas_call(
        _proj_kernel,
        grid=(S // tm,),
        in_specs=[
            pl.BlockSpec((tm, D_MODEL), lambda i: (i, 0)),
            _resident((1, D_MODEL), lambda i: (0, 0)),
            _resident((None, D_MODEL, PROJ_N_COLS), lambda i: (layer, 0, 0)),
            _resident((None, PROJ_T_ROWS, D_MODEL), lambda i: (layer, 0, 0)),
        ],
        out_specs=[
            pl.BlockSpec((tm, PROJ_N_COLS), lambda i: (i, 0)),
            pl.BlockSpec((1, PROJ_T_ROWS, tm), lambda i: (i, 0, 0)),
        ],
        out_shape=[
            jax.ShapeDtypeStruct((S, PROJ_N_COLS), BF16),
            jax.ShapeDtypeStruct((S // tm, PROJ_T_ROWS, tm), BF16),
        ],
        compiler_params=_params("parallel"),
        name="proj",
    )(x, g, wn, wt)


def _staggered(n, stage_a, stage_b):
    cur = stage_a(0)
    for i in range(n):
        nxt = stage_a(i + 1) if i + 1 < n else None
        stage_b(i, cur)
        cur = nxt


def _na_kernel(q_ref, k0_ref, k1_ref, k2_ref, v0_ref, v1_ref, v2_ref, b_ref, o_ref):
    nq = NA_BLOCK_ROWS * GRID_W
    q = q_ref[...].astype(F32)
    k3 = jnp.concatenate([k0_ref[...], k1_ref[...], k2_ref[...]], axis=0)
    v3 = jnp.concatenate([v0_ref[...], v1_ref[...], v2_ref[...]], axis=0)
    first = lax.broadcasted_iota(jnp.int32, (nq, LANES), 1) < HEAD_DIM
    outs = [None] * NA_HEADS

    def logits(head):
        cols = slice((head // 2) * LANES, (head // 2 + 1) * LANES)
        keep = first if head % 2 == 0 else jnp.logical_not(first)
        qm = jnp.where(keep, q[:, cols], 0.0).astype(BF16)
        s = lax.dot_general(qm, k3[:, cols], (((1,), (1,)), ((), ())), preferred_element_type=F32)
        return s + b_ref[0, head]

    def attend(head, s):
        cols = slice((head // 2) * LANES, (head // 2 + 1) * LANES)
        m = jnp.max(s, axis=-1, keepdims=True)
        p = jnp.exp2(s - m)
        l = jnp.sum(p, axis=-1, keepdims=True)
        outs[head] = jnp.dot(p.astype(BF16), v3[:, cols], preferred_element_type=F32) / l

    _staggered(NA_HEADS, logits, attend)
    for pair in range(NA_HEADS // 2):
        o_ref[:, pair * LANES:(pair + 1) * LANES] = jnp.where(
            first, outs[2 * pair], outs[2 * pair + 1]).astype(o_ref.dtype)


def _na(proj, bias, layer):
    S = proj.shape[0]
    nq = NA_BLOCK_ROWS * GRID_W
    nb = S // nq
    width = NA_HEADS * HEAD_DIM
    qcol, kcol, vcol = N_QA // width, N_KA // width, N_VA // width

    def window(col, j):
        return pl.BlockSpec((nq, width), lambda b: (jnp.clip(b - 1, 0, nb - 3) + j, col))

    return pl.pallas_call(
        _na_kernel,
        grid=(nb,),
        in_specs=[
            pl.BlockSpec((nq, width), lambda b: (b, qcol)),
            window(kcol, 0), window(kcol, 1), window(kcol, 2),
            window(vcol, 0), window(vcol, 1), window(vcol, 2),
            pl.BlockSpec((None, 1, NA_HEADS, nq, 3 * nq),
                         lambda b: (layer, jnp.where(b == 0, 0, jnp.where(b == nb - 1, 2, 1)), 0, 0, 0)),
        ],
        out_specs=pl.BlockSpec((nq, width), lambda b: (b, 0)),
        out_shape=jax.ShapeDtypeStruct((S, width), BF16),
        compiler_params=_params("parallel"),
        name="na_attn",
    )(proj, proj, proj, proj, proj, proj, proj, bias)


def _na_bias_tables(rpb, rows):
    br, wr = NA_BLOCK_ROWS, NA_WIN_BLOCK_ROWS
    half = NA_WIN_ROWS // 2
    c = np.arange(GRID_W)
    cs = np.clip(c - NA_WIN_COLS // 2, 0, GRID_W - NA_WIN_COLS)
    col_ok = (c[None, :] >= cs[:, None]) & (c[None, :] < cs[:, None] + NA_WIN_COLS)
    pad = GRID_W - NA_WIN_COLS
    vals = jnp.pad(rpb.astype(F32), ((0, 0), (0, 0), (0, 0), (pad, pad)))
    t_all = jnp.where(col_ok, _toeplitz(vals, GRID_W, GRID_W), NEG_INF)
    L, H = rpb.shape[0], rpb.shape[1]
    n_dr = 2 * NA_WIN_ROWS - 1
    seq = jnp.moveaxis(t_all, 2, 3).reshape(L, H, GRID_W, n_dr * GRID_W)
    seq = jnp.pad(seq, ((0, 0), (0, 0), (0, 0), (wr * GRID_W, wr * GRID_W)), constant_values=NEG_INF)
    dr_of_lane = np.arange((n_dr + 2 * wr) * GRID_W) // GRID_W - wr
    types = []
    for b in (0, 1, rows // br - 1):
        win0 = int(np.clip(b * br - half, 0, rows - wr))
        row_blocks = []
        for j in range(br):
            r = b * br + j
            rs = int(np.clip(r - half, 0, rows - NA_WIN_ROWS))
            w_lo, dr_lo = rs - win0, rs - r + (NA_WIN_ROWS - 1)
            inside = (dr_of_lane >= dr_lo) & (dr_of_lane < dr_lo + NA_WIN_ROWS)
            start = (wr + dr_lo - w_lo) * GRID_W
            row_blocks.append(jnp.where(inside, seq, NEG_INF)[..., start:start + wr * GRID_W])
        types.append(jnp.concatenate(row_blocks, axis=2))
    return jnp.stack(types, axis=1)


def _t5_bucket(rel):
    half = T5_BUCKETS // 2
    max_exact = half // 2
    ret = (rel > 0).astype(jnp.int32) * half
    n = jnp.abs(rel)
    nf = jnp.maximum(n, 1).astype(jnp.float32)
    large = max_exact + (jnp.log(nf / max_exact) / math.log(T5_MAX_DIST / max_exact)
                         * (half - max_exact)).astype(jnp.int32)
    large = jnp.minimum(large, half - 1)
    return ret + jnp.where(n < max_exact, n, large)


def _toeplitz(vals, n_rows, n_cols):
    length = n_rows + n_cols - 1
    lead = vals.shape[:-1]
    w = jnp.concatenate([vals, jnp.zeros(lead + (1,), vals.dtype)], axis=-1)
    flat = jnp.tile(w, (1,) * len(lead) + (n_rows,))[..., : n_rows * length]
    return flat.reshape(lead + (n_rows, length))[..., n_rows - 1:]


def _rel_bias(table, lo, hi):
    rel = jnp.arange(lo, hi + 1, dtype=jnp.int32)
    return jnp.transpose(table.astype(F32)[_t5_bucket(rel)], (1, 0))


def _diff_kernel(scal_ref, q_ref, k_ref, v_ref, b_ref, g_ref, lq_ref, o_ref,
                 qa_ref, qb_ref, s0_ref, s1_ref, t0_ref, t1_ref, p0_ref, p1_ref, a0_ref, a1_ref,
                 m_ref, acc_ref, *, nk):
    h = pl.program_id(0)
    qi = pl.program_id(1)
    tq, tk = DIFF_TQ, DIFF_TK
    two_d = 2 * HEAD_DIM

    q = q_ref[0].astype(F32)
    row = lax.broadcasted_iota(jnp.int32, (two_d, tq), 0)
    qa_ref[...] = jnp.where(row < HEAD_DIM, q, 0.0).astype(BF16)
    qb_ref[...] = jnp.where(row >= HEAD_DIM, q, 0.0).astype(BF16)
    m_ref[...] = jnp.full(m_ref.shape, NEG_INF, F32)
    acc_ref[...] = jnp.zeros(acc_ref.shape, F32)
    c_left = scal_ref[h]
    c_right = scal_ref[DIFF_HEADS + h]
    ones_rows = jnp.where(lax.broadcasted_iota(jnp.int32, (DIFF_VROWS - two_d, tk), 0) == 0,
                          1.0, 0.0).astype(BF16)

    chunks = [(c, slice(j, j + DIFF_CHUNK)) for c in range(2) for j in range(0, tq, DIFF_CHUNK)]
    qm_refs = (qa_ref, qb_ref)
    even = (s0_ref, t0_ref, p0_ref, a0_ref)
    odd = (s1_ref, t1_ref, p1_ref, a1_ref)

    def scores(kb, bias_fn, bufs, c, cols):
        s_ref, t_ref = bufs[0], bufs[1]
        s = jnp.dot(kb, qm_refs[c][:, cols], preferred_element_type=F32)
        if bias_fn is not None:
            s = s + bias_fn(cols)
        s_ref[c, :, cols] = s
        t_ref[c, :, cols] = jnp.max(s, axis=0, keepdims=True)

    def softmax(shift, bufs, c, cols):
        s_ref, t_ref, p_ref, a_ref = bufs
        m_old = m_ref[c, :, cols]
        m_new = jnp.maximum(m_old, t_ref[c, :, cols] + shift)
        a_ref[c, :, cols] = jnp.exp2(m_old - m_new)
        p_ref[c, :, cols] = jnp.exp2(s_ref[c, :, cols] - (m_new - shift)).astype(BF16)
        m_ref[c, :, cols] = m_new

    def values(vt, bufs, c, cols):
        p_ref, a_ref = bufs[2], bufs[3]
        acc_ref[c, :, cols] = (a_ref[c, :, cols] * acc_ref[c, :, cols]
                               + jnp.dot(vt, p_ref[c, :, cols], preferred_element_type=F32))

    def block(nxt, cur, prv):
        for c, cols in chunks:
            if nxt is not None:
                kb = k_ref[pl.ds(pl.multiple_of(nxt[0] * tk, tk), tk), :]
                scores(kb, nxt[1], nxt[2], c, cols)
            if cur is not None:
                softmax(cur[0], cur[1], c, cols)
            if prv is not None:
                vt = jnp.concatenate([v_ref[prv[0]], ones_rows], axis=0)
                values(vt, prv[1], c, cols)

    has_left = qi >= 1
    has_right = qi + 1 < nk
    k_left = jnp.where(has_left, qi - 1, nk - 1)
    k_right = jnp.where(has_right, qi + 1, 0)

    def near_bias(index, exists, const):
        def fn(cols):
            bias = b_ref[0, index, :, cols]
            return bias if exists is None else jnp.where(exists, bias, const)
        return fn

    first_far = jnp.where(has_right, 0, 1)
    win_lo = jnp.maximum(qi - 1, 0)
    win_len = jnp.minimum(qi + 1, nk - 1) - win_lo + 1
    n_far = nk - 3

    def far(f):
        ki = first_far + f
        ki = jnp.minimum(jnp.where(ki < win_lo, ki, ki + win_len), nk - 1)
        return ki, jnp.where(ki < qi, c_left, c_right)

    far0, shift0 = far(0)
    far1, _ = far(1)
    block((qi, near_bias(1, None, None), even), None, None)
    block((k_left, near_bias(0, has_left, c_right), odd), (0.0, even), None)
    block((k_right, near_bias(2, has_right, c_left), even), (0.0, odd), (qi, even))
    block((far0, None, odd), (0.0, even), (k_left, odd))
    block((far1, None, even), (shift0, odd), (k_right, even))

    def pair(f, k_prev):
        k_a, shift_a = far(f)
        k_b, shift_b = far(f + 1)
        k_c, _ = far(f + 2)
        block((k_b, None, odd), (shift_a, even), (k_prev, odd))
        block((k_c, None, even), (shift_b, odd), (k_a, even))
        return k_b

    n_pairs = (n_far - 1) // 2
    per_trip = max(d for d in range(1, DIFF_MAX_PAIRS_PER_TRIP + 1) if n_pairs % d == 0)

    def pairs(j, k_prev):
        for i in range(per_trip):
            k_prev = pair(1 + 2 * (per_trip * j + i), k_prev)
        return k_prev

    assert n_far % 2 == 1
    k_last = lax.fori_loop(0, n_pairs // per_trip, pairs, far0)
    block(None, None, (k_last, odd))

    lam_init = scal_ref[2 * DIFF_HEADS]
    lq = lq_ref[...]
    lam = (jnp.exp(jnp.sum(lq[0:1] * lq[1:2], axis=-1, keepdims=True))
           - jnp.exp(jnp.sum(lq[2:3] * lq[3:4], axis=-1, keepdims=True)) + lam_init)
    acc0, acc1 = acc_ref[0], acc_ref[1]
    o = (acc0[:two_d] / acc0[two_d:two_d + 1]
         - lam * (acc1[:two_d] / acc1[two_d:two_d + 1]))
    y = o * lax.rsqrt(jnp.mean(o * o, axis=0, keepdims=True) + EPS) * g_ref[...]
    o_ref[...] = jnp.transpose(y * (1.0 - lam_init)).astype(o_ref.dtype)


def _diff(scal, proj, proj_t, bias_t, subln_g, lq):
    S = proj.shape[0]
    tq, tk = DIFF_TQ, DIFF_TK
    nk = S // tk
    assert nk % 2 == 0 and tq == ROW_TILE and tk == ROW_TILE
    two_d = 2 * HEAD_DIM
    kcol = N_KD // LANES
    qrow, vrow = T_QD // two_d, T_VD // two_d
    return pl.pallas_call(
        functools.partial(_diff_kernel, nk=nk),
        grid=(DIFF_HEADS, S // tq),
        in_specs=[
            pl.BlockSpec(memory_space=pltpu.SMEM),
            pl.BlockSpec((1, two_d, tq), lambda h, i: (i, qrow + h, 0)),
            pl.BlockSpec((S, LANES), lambda h, i: (0, kcol + h)),
            pl.BlockSpec((nk, two_d, tk), lambda h, i: (0, vrow + h, 0)),
            pl.BlockSpec((1, 3, tk, tq), lambda h, i: (h, 0, 0, 0)),
            _resident((two_d, 1), lambda h, i: (0, 0)),
            _resident((4, HEAD_DIM), lambda h, i: (0, 0)),
        ],
        out_specs=pl.BlockSpec((tq, two_d), lambda h, i: (i, h)),
        out_shape=jax.ShapeDtypeStruct((S, DIFF_HEADS * two_d), BF16),
        scratch_shapes=[
            pltpu.VMEM((two_d, tq), BF16),
            pltpu.VMEM((two_d, tq), BF16),
            pltpu.VMEM((2, tk, tq), F32),
            pltpu.VMEM((2, tk, tq), F32),
            pltpu.VMEM((2, 1, tq), F32),
            pltpu.VMEM((2, 1, tq), F32),
            pltpu.VMEM((2, tk, tq), BF16),
            pltpu.VMEM((2, tk, tq), BF16),
            pltpu.VMEM((2, 1, tq), F32),
            pltpu.VMEM((2, 1, tq), F32),
            pltpu.VMEM((2, 1, tq), F32),
            pltpu.VMEM((2, DIFF_VROWS, tq), F32),
        ],
        compiler_params=_params("parallel", "arbitrary"),
        name="diff_attn",
    )(scal, proj_t, proj, proj_t, bias_t, subln_g, lq)


def _diff_bias_tiles(table):
    tq, tk = DIFF_TQ, DIFF_TK
    tiles = []
    for d in (-1, 0, 1):
        vals = _rel_bias(table, d * tk - (tq - 1), d * tk + tk - 1)
        tiles.append(jnp.swapaxes(_toeplitz(vals, tq, tk), -1, -2))
    return jnp.stack(tiles, axis=1)


def _gqa_kernel(q_ref, kp_ref, kc_ref, kn_ref, vp_ref, vc_ref, vn_ref, b_ref, sink_ref, o_ref, *, n_blocks):
    blk = pl.program_id(1)
    sub = GQA_QB // WINDOW
    kw = 3 * WINDOW
    width = GQA_GROUP * HEAD_DIM
    qt = q_ref[0].astype(F32)
    kx = jnp.concatenate([kp_ref[...], kc_ref[...], kn_ref[...]], axis=0)
    vt = jnp.concatenate([vp_ref[0], vc_ref[0], vn_ref[0]], axis=1)
    row_head = lax.broadcasted_iota(jnp.int32, (width, WINDOW), 0) // HEAD_DIM
    krow = lax.broadcasted_iota(jnp.int32, (kw, 1), 0)
    ones_rows = jnp.where(lax.broadcasted_iota(jnp.int32, (BF16_ROWS, kw), 0) == 0, 1.0, 0.0).astype(BF16)
    sink = sink_ref[0]
    outs = [None] * sub

    def logits(i):
        qi = qt[:, i * WINDOW:(i + 1) * WINDOW]
        qs = jnp.concatenate([jnp.where(row_head == g, qi, 0.0).astype(BF16) for g in range(GQA_GROUP)], axis=1)
        s = jnp.dot(kx[i * WINDOW:i * WINDOW + kw], qs, preferred_element_type=F32) + b_ref[0]
        if i == 0:
            s = jnp.where((krow < WINDOW) & (blk == 0), NEG_INF, s)
        if i == sub - 1:
            s = jnp.where((krow >= 2 * WINDOW) & (blk == n_blocks - 1), NEG_INF, s)
        return s

    def attend(i, s):
        m = jnp.maximum(jnp.max(s, axis=0, keepdims=True), sink)
        p = jnp.exp2(s - m).astype(BF16)
        v_ext = jnp.concatenate([vt[:, i * WINDOW:i * WINDOW + kw], ones_rows], axis=0)
        o = jnp.dot(v_ext, p, preferred_element_type=F32)
        o = o[:HEAD_DIM] / (o[HEAD_DIM:HEAD_DIM + 1] + jnp.exp2(sink - m))
        outs[i] = jnp.concatenate([o[:, g * WINDOW:(g + 1) * WINDOW] for g in range(GQA_GROUP)], axis=0)

    _staggered(sub, logits, attend)
    o_ref[...] = jnp.transpose(jnp.concatenate(outs, axis=1)).astype(o_ref.dtype)


def _gqa(proj, proj_t, bias_t, sink_lanes):
    S = proj.shape[0]
    qb, kw = GQA_QB, 3 * WINDOW
    assert qb == ROW_TILE
    width = GQA_GROUP * HEAD_DIM
    nb = S // qb
    sub = qb // WINDOW
    last = S // WINDOW - 1
    kcol = N_KC // width
    qrow = T_QC // width
    vrow = T_VC // HEAD_DIM

    def prev_blk(b):
        return jnp.maximum(b * sub - 1, 0)

    def next_blk(b):
        return jnp.minimum(b * sub + sub, last)

    return pl.pallas_call(
        functools.partial(_gqa_kernel, n_blocks=nb),
        grid=(GQA_KV_HEADS, nb),
        in_specs=[
            pl.BlockSpec((1, width, qb), lambda kv, b: (b, qrow + kv, 0)),
            pl.BlockSpec((WINDOW, width), lambda kv, b: (prev_blk(b), kcol + kv)),
            pl.BlockSpec((qb, width), lambda kv, b: (b, kcol + kv)),
            pl.BlockSpec((WINDOW, width), lambda kv, b: (next_blk(b), kcol + kv)),
            pl.BlockSpec((1, HEAD_DIM, WINDOW), lambda kv, b: (prev_blk(b) // sub, vrow + kv, prev_blk(b) % sub)),
            pl.BlockSpec((1, HEAD_DIM, qb), lambda kv, b: (b, vrow + kv, 0)),
            pl.BlockSpec((1, HEAD_DIM, WINDOW), lambda kv, b: (next_blk(b) // sub, vrow + kv, next_blk(b) % sub)),
            pl.BlockSpec((1, kw, GQA_GROUP * WINDOW), lambda kv, b: (kv, 0, 0)),
            pl.BlockSpec((1, 1, GQA_GROUP * WINDOW), lambda kv, b: (kv, 0, 0)),
        ],
        out_specs=pl.BlockSpec((qb, width), lambda kv, b: (b, kv)),
        out_shape=jax.ShapeDtypeStruct((S, GQA_Q_HEADS * HEAD_DIM), BF16),
        compiler_params=_params("parallel", "arbitrary"),
        name="gqa_attn",
    )(proj_t, proj, proj, proj, proj_t, proj_t, proj_t, bias_t, sink_lanes)


def _gqa_bias(table):
    qb, kw = WINDOW, 3 * WINDOW
    vals = _rel_bias(table, -WINDOW - (qb - 1), kw - 1 - WINDOW)
    rel = jnp.arange(-WINDOW - (qb - 1), kw - WINDOW, dtype=jnp.int32)
    vals = jnp.where(jnp.abs(rel)[None, :] <= WINDOW, vals, NEG_INF)
    bias = _toeplitz(vals, qb, kw)
    bias = bias.reshape(GQA_KV_HEADS, GQA_GROUP, qb, kw)
    return jnp.transpose(bias, (0, 3, 1, 2)).reshape(GQA_KV_HEADS, kw, GQA_GROUP * qb)


def _merge_kernel(x_ref, ya_ref, yb_ref, yc_ref, g_ref, wg_ref, bg_ref, wb_ref, wo_ref, o_ref):
    x = x_ref[...]
    h = _rms(x, g_ref[...]).astype(BF16)
    gates = _sigmoid(jnp.dot(h, wg_ref[...], preferred_element_type=F32) + bg_ref[...])
    merged = None
    for i, y_ref in enumerate((ya_ref, yb_ref, yc_ref)):
        t = gates[:, i * D_MODEL:(i + 1) * D_MODEL] * jnp.dot(y_ref[...], wb_ref[i], preferred_element_type=F32)
        merged = t if merged is None else merged + t
    o_ref[...] = x + jnp.dot(merged.astype(BF16), wo_ref[...], preferred_element_type=F32)


def _merge(x, ya, yb, yc, g, wg, bg, wb, wo, layer):
    S = x.shape[0]
    tm = ROW_TILE
    branch = pl.BlockSpec((tm, BRANCH_WIDTH), lambda i: (i, 0))
    return pl.pallas_call(
        _merge_kernel,
        grid=(S // tm,),
        in_specs=[
            pl.BlockSpec((tm, D_MODEL), lambda i: (i, 0)),
            branch, branch, branch,
            _resident((1, D_MODEL), lambda i: (0, 0)),
            _resident((None, D_MODEL, N_BRANCHES * D_MODEL), lambda i: (layer, 0, 0)),
            _resident((1, N_BRANCHES * D_MODEL), lambda i: (0, 0)),
            _resident((None, N_BRANCHES, BRANCH_WIDTH, D_MODEL), lambda i: (layer, 0, 0, 0)),
            _resident((None, D_MODEL, D_MODEL), lambda i: (layer, 0, 0)),
        ],
        out_specs=pl.BlockSpec((tm, D_MODEL), lambda i: (i, 0)),
        out_shape=jax.ShapeDtypeStruct((S, D_MODEL), F32),
        compiler_params=_params("parallel"),
        name="merge",
    )(x, ya, yb, yc, g, wg, bg, wb, wo)


def kernel(x, w_in, w_branch, w_gate, b_gate, w_o, norm_g, final_g, ffn_w_gate, ffn_w_up, ffn_w_down,
           na_rpb, diff_lambda, diff_subln_g, gqa_sink, rel_bias_table):
    B, S, _ = x.shape
    assert B == 1 and S % (2 * DIFF_TK) == 0 and S // (NA_BLOCK_ROWS * GRID_W) >= 4
    depth = w_in.shape[0]
    two_d = 2 * HEAD_DIM

    col_scale = np.ones((W_IN_COLS,), np.float32)
    for off in (QA_OFF, QD_OFF, QC_OFF):
        col_scale[off:off + KA_OFF - QA_OFF] = HEAD_DIM ** -0.5 * LOG2E
    w_in_s = (w_in * col_scale).astype(BF16)
    w_kc_tiled = jnp.tile(w_in_s[:, :, KC_OFF:VC_OFF].reshape(depth, D_MODEL, GQA_KV_HEADS, 1, HEAD_DIM),
                          (1, 1, 1, GQA_GROUP, 1)).reshape(depth, D_MODEL, GQA_Q_HEADS * HEAD_DIM)
    w_in_n = jnp.concatenate([w_in_s[:, :, :QD_OFF], w_in_s[:, :, KD_OFF:VD_OFF], w_kc_tiled], axis=2)
    w_in_t = jnp.swapaxes(jnp.concatenate(
        [w_in_s[:, :, QD_OFF:KD_OFF], w_in_s[:, :, VD_OFF:QC_OFF], w_in_s[:, :, QC_OFF:KC_OFF],
         w_in_s[:, :, VC_OFF:]], axis=2), 1, 2)
    w_branch_b = w_branch.astype(BF16)
    w_gate_b = w_gate.astype(BF16)
    w_o_b = w_o.astype(BF16)
    ffn_wg_b = ffn_w_gate.astype(BF16)
    ffn_wu_b = ffn_w_up.astype(BF16)
    ffn_wd_b = ffn_w_down.astype(BF16)

    na_bias = _na_bias_tables(na_rpb * LOG2E, S // GRID_W)
    diff_table = rel_bias_table[:, :DIFF_HEADS]
    diff_bias = _diff_bias_tiles(diff_table) * LOG2E
    far = _rel_bias(diff_table, -T5_MAX_DIST, T5_MAX_DIST) * LOG2E
    gqa_bias = _gqa_bias(rel_bias_table[:, DIFF_HEADS:]) * LOG2E
    final_row = final_g.reshape(1, D_MODEL).astype(F32)

    xs = x.reshape(S, D_MODEL)
    for l in range(depth):
        xs = _ffn(xs, norm_g[l, 0].reshape(1, D_MODEL), ffn_wg_b, ffn_wu_b, ffn_wd_b, final_row, l, 0, False)
        g1 = norm_g[l, 1].reshape(1, D_MODEL)
        proj, proj_t = _proj(xs, g1, w_in_n, w_in_t, l)

        ya = _na(proj, na_bias, l)

        lam_init = 0.8 - 0.6 * math.exp(-0.3 * l)
        scal = jnp.concatenate([far[:, 0], far[:, -1], jnp.full((1,), lam_init, F32)])
        yb = _diff(scal, proj, proj_t, diff_bias, diff_subln_g[l].reshape(two_d, 1).astype(F32),
                   diff_lambda[l].astype(F32))

        sink_lanes = jnp.repeat(gqa_sink[l].astype(F32) * LOG2E, WINDOW).reshape(
            GQA_KV_HEADS, 1, GQA_GROUP * WINDOW)
        yc = _gqa(proj, proj_t, gqa_bias, sink_lanes)

        xs = _merge(xs, ya, yb, yc, g1, w_gate_b, b_gate[l].reshape(1, -1).astype(F32), w_branch_b, w_o_b, l)
        xs = _ffn(xs, norm_g[l, 2].reshape(1, D_MODEL), ffn_wg_b, ffn_wu_b, ffn_wd_b, final_row, l, 1,
                  l == depth - 1)
    return xs.reshape(B, S, D_MODEL)
```

```python
import functools
import math

import jax
import jax.numpy as jnp
import numpy as np
from jax import lax
from jax.experimental import pallas as pl
from jax.experimental.pallas import tpu as pltpu

D_MODEL = 1024
HEAD_DIM = 64
NA_HEADS = 8
DIFF_HEADS = 4
GQA_Q_HEADS = 8
GQA_KV_HEADS = 2
GQA_GROUP = GQA_Q_HEADS // GQA_KV_HEADS
BRANCH_WIDTH = 512
N_BRANCHES = 3
D_FF = 2816
GRID_W = 64
NA_WIN_ROWS = 8
NA_WIN_COLS = 16
WINDOW = 128
T5_BUCKETS = 32
T5_MAX_DIST = 128
NEG_INF = -1e30
EPS = 1e-6

QA_OFF, KA_OFF, VA_OFF = 0, 512, 1024
QD_OFF, KD_OFF, VD_OFF = 1536, 2048, 2560
QC_OFF, KC_OFF, VC_OFF = 3072, 3584, 3712
W_IN_COLS = 3840
N_QA, N_KA, N_VA, N_KD, N_KC = 0, 512, 1024, 1536, 2048
PROJ_N_COLS = 2560
T_QD, T_VD, T_QC, T_VC = 0, 512, 1024, 1536
PROJ_T_ROWS = 1664

LANES = 128
BF16_ROWS = 16
VMEM_LIMIT = 56 * 1024 * 1024

ROW_TILE = 512
NA_BLOCK_ROWS = 4
NA_WIN_BLOCK_ROWS = 12
DIFF_TQ = 512
DIFF_TK = 512
DIFF_CHUNK = 256
DIFF_MAX_PAIRS_PER_TRIP = 7
DIFF_VROWS = 2 * HEAD_DIM + BF16_ROWS
LOG2E = math.log2(math.e)
GQA_QB = 512

F32 = jnp.float32
BF16 = jnp.bfloat16


def _resident(shape, index_map):
    return pl.BlockSpec(shape, index_map, pipeline_mode=pl.Buffered(1))


def _params(*sem):
    return pltpu.CompilerParams(dimension_semantics=sem, vmem_limit_bytes=VMEM_LIMIT)


def _rms(x, g):
    return x * lax.rsqrt(jnp.mean(x * x, axis=-1, keepdims=True) + EPS) * g


def _sigmoid(z):
    return 1.0 / (1.0 + jnp.exp(-z))


def _ffn_kernel(x_ref, g_ref, wg_ref, wu_ref, wd_ref, fg_ref, o_ref, *, final):
    x = x_ref[...]
    h = _rms(x, g_ref[...]).astype(BF16)
    gate = jnp.dot(h, wg_ref[...], preferred_element_type=F32)
    up = jnp.dot(h, wu_ref[...], preferred_element_type=F32)
    a = (gate * _sigmoid(gate) * up).astype(BF16)
    y = x + 0.5 * jnp.dot(a, wd_ref[...], preferred_element_type=F32)
    if final:
        y = _rms(y, fg_ref[...])
    o_ref[...] = y


def _ffn(x, g, wg, wu, wd, fg, layer, which, final):
    S = x.shape[0]
    tm = ROW_TILE
    pick = lambda i: (layer, which, 0, 0)
    return pl.pallas_call(
        functools.partial(_ffn_kernel, final=final),
        grid=(S // tm,),
        in_specs=[
            pl.BlockSpec((tm, D_MODEL), lambda i: (i, 0)),
            _resident((1, D_MODEL), lambda i: (0, 0)),
            _resident((None, None, D_MODEL, D_FF), pick),
            _resident((None, None, D_MODEL, D_FF), pick),
            _resident((None, None, D_FF, D_MODEL), pick),
            _resident((1, D_MODEL), lambda i: (0, 0)),
        ],
        out_specs=pl.BlockSpec((tm, D_MODEL), lambda i: (i, 0)),
        out_shape=jax.ShapeDtypeStruct((S, D_MODEL), F32),
        compiler_params=_params("parallel"),
        name="ffn",
    )(x, g, wg, wu, wd, fg)


def _proj_kernel(x_ref, g_ref, wn_ref, wt_ref, on_ref, ot_ref):
    h = _rms(x_ref[...], g_ref[...]).astype(BF16)
    on_ref[...] = jnp.dot(h, wn_ref[...], preferred_element_type=F32).astype(BF16)
    ot_ref[0] = lax.dot_general(wt_ref[...], h, (((1,), (1,)), ((), ())),
                                preferred_element_type=F32).astype(BF16)


def _proj(x, g, wn, wt, layer):
    S = x.shape[0]
    tm = ROW_TILE
    return pl.pallas_call(
        _proj_kernel,
        grid=(S // tm,),
        in_specs=[
            pl.BlockSpec((tm, D_MODEL), lambda i: (i, 0)),
            _resident((1, D_MODEL), lambda i: (0, 0)),
            _resident((None, D_MODEL, PROJ_N_COLS), lambda i: (layer, 0, 0)),
            _resident((None, PROJ_T_ROWS, D_MODEL), lambda i: (layer, 0, 0)),
        ],
        out_specs=[
            pl.BlockSpec((tm, PROJ_N_COLS), lambda i: (i, 0)),
            pl.BlockSpec((1, PROJ_T_ROWS, tm), lambda i: (i, 0, 0)),
        ],
        out_shape=[
            jax.ShapeDtypeStruct((S, PROJ_N_COLS), BF16),
            jax.ShapeDtypeStruct((S // tm, PROJ_T_ROWS, tm), BF16),
        ],
        compiler_params=_params("parallel"),
        name="proj",
    )(x, g, wn, wt)


def _staggered(n, stage_a, stage_b):
    cur = stage_a(0)
    for i in range(n):
        nxt = stage_a(i + 1) if i + 1 < n else None
        stage_b(i, cur)
        cur = nxt


def _na_kernel(q_ref, k0_ref, k1_ref, k2_ref, v0_ref, v1_ref, v2_ref, b_ref, o_ref):
    nq = NA_BLOCK_ROWS * GRID_W
    q = q_ref[...].astype(F32)
    k3 = jnp.concatenate([k0_ref[...], k1_ref[...], k2_ref[...]], axis=0)
    v3 = jnp.concatenate([v0_ref[...], v1_ref[...], v2_ref[...]], axis=0)
    first = lax.broadcasted_iota(jnp.int32, (nq, LANES), 1) < HEAD_DIM
    outs = [None] * NA_HEADS

    def logits(head):
        cols = slice((head // 2) * LANES, (head // 2 + 1) * LANES)
        keep = first if head % 2 == 0 else jnp.logical_not(first)
        qm = jnp.where(keep, q[:, cols], 0.0).astype(BF16)
        s = lax.dot_general(qm, k3[:, cols], (((1,), (1,)), ((), ())), preferred_element_type=F32)
        return s + b_ref[0, head]

    def attend(head, s):
        cols = slice((head // 2) * LANES, (head // 2 + 1) * LANES)
        m = jnp.max(s, axis=-1, keepdims=True)
        p = jnp.exp2(s - m)
        l = jnp.sum(p, axis=-1, keepdims=True)
        outs[head] = jnp.dot(p.astype(BF16), v3[:, cols], preferred_element_type=F32) / l

    _staggered(NA_HEADS, logits, attend)
    for pair in range(NA_HEADS // 2):
        o_ref[:, pair * LANES:(pair + 1) * LANES] = jnp.where(
            first, outs[2 * pair], outs[2 * pair + 1]).astype(o_ref.dtype)


def _na(proj, bias, layer):
    S = proj.shape[0]
    nq = NA_BLOCK_ROWS * GRID_W
    nb = S // nq
    width = NA_HEADS * HEAD_DIM
    qcol, kcol, vcol = N_QA // width, N_KA // width, N_VA // width

    def window(col, j):
        return pl.BlockSpec((nq, width), lambda b: (jnp.clip(b - 1, 0, nb - 3) + j, col))

    return pl.pallas_call(
        _na_kernel,
        grid=(nb,),
        in_specs=[
            pl.BlockSpec((nq, width), lambda b: (b, qcol)),
            window(kcol, 0), window(kcol, 1), window(kcol, 2),
            window(vcol, 0), window(vcol, 1), window(vcol, 2),
            pl.BlockSpec((None, 1, NA_HEADS, nq, 3 * nq),
                         lambda b: (layer, jnp.where(b == 0, 0, jnp.where(b == nb - 1, 2, 1)), 0, 0, 0)),
        ],
        out_specs=pl.BlockSpec((nq, width), lambda b: (b, 0)),
        out_shape=jax.ShapeDtypeStruct((S, width), BF16),
        compiler_params=_params("parallel"),
        name="na_attn",
    )(proj, proj, proj, proj, proj, proj, proj, bias)


def _na_bias_tables(rpb, rows):
    br, wr = NA_BLOCK_ROWS, NA_WIN_BLOCK_ROWS
    half = NA_WIN_ROWS // 2
    c = np.arange(GRID_W)
    cs = np.clip(c - NA_WIN_COLS // 2, 0, GRID_W - NA_WIN_COLS)
    col_ok = (c[None, :] >= cs[:, None]) & (c[None, :] < cs[:, None] + NA_WIN_COLS)
    pad = GRID_W - NA_WIN_COLS
    vals = jnp.pad(rpb.astype(F32), ((0, 0), (0, 0), (0, 0), (pad, pad)))
    t_all = jnp.where(col_ok, _toeplitz(vals, GRID_W, GRID_W), NEG_INF)
    L, H = rpb.shape[0], rpb.shape[1]
    n_dr = 2 * NA_WIN_ROWS - 1
    seq = jnp.moveaxis(t_all, 2, 3).reshape(L, H, GRID_W, n_dr * GRID_W)
    seq = jnp.pad(seq, ((0, 0), (0, 0), (0, 0), (wr * GRID_W, wr * GRID_W)), constant_values=NEG_INF)
    dr_of_lane = np.arange((n_dr + 2 * wr) * GRID_W) // GRID_W - wr
    types = []
    for b in (0, 1, rows // br - 1):
        win0 = int(np.clip(b * br - half, 0, rows - wr))
        row_blocks = []
        for j in range(br):
            r = b * br + j
            rs = int(np.clip(r - half, 0, rows - NA_WIN_ROWS))
            w_lo, dr_lo = rs - win0, rs - r + (NA_WIN_ROWS - 1)
            inside = (dr_of_lane >= dr_lo) & (dr_of_lane < dr_lo + NA_WIN_ROWS)
            start = (wr + dr_lo - w_lo) * GRID_W
            row_blocks.append(jnp.where(inside, seq, NEG_INF)[..., start:start + wr * GRID_W])
        types.append(jnp.concatenate(row_blocks, axis=2))
    return jnp.stack(types, axis=1)


def _t5_bucket(rel):
    half = T5_BUCKETS // 2
    max_exact = half // 2
    ret = (rel > 0).astype(jnp.int32) * half
    n = jnp.abs(rel)
    nf = jnp.maximum(n, 1).astype(jnp.float32)
    large = max_exact + (jnp.log(nf / max_exact) / math.log(T5_MAX_DIST / max_exact)
                         * (half - max_exact)).astype(jnp.int32)
    large = jnp.minimum(large, half - 1)
    return ret + jnp.where(n < max_exact, n, large)


def _toeplitz(vals, n_rows, n_cols):
    length = n_rows + n_cols - 1
    lead = vals.shape[:-1]
    w = jnp.concatenate([vals, jnp.zeros(lead + (1,), vals.dtype)], axis=-1)
    flat = jnp.tile(w, (1,) * len(lead) + (n_rows,))[..., : n_rows * length]
    return flat.reshape(lead + (n_rows, length))[..., n_rows - 1:]


def _rel_bias(table, lo, hi):
    rel = jnp.arange(lo, hi + 1, dtype=jnp.int32)
    return jnp.transpose(table.astype(F32)[_t5_bucket(rel)], (1, 0))


def _diff_kernel(scal_ref, q_ref, k_ref, v_ref, b_ref, g_ref, lq_ref, o_ref,
                 qa_ref, qb_ref, s0_ref, s1_ref, t0_ref, t1_ref,
                 m_ref, acc_ref, *, nk):
    h = pl.program_id(0)
    qi = pl.program_id(1)
    tq, tk = DIFF_TQ, DIFF_TK
    two_d = 2 * HEAD_DIM

    q = q_ref[0].astype(F32)
    row = lax.broadcasted_iota(jnp.int32, (two_d, tq), 0)
    qa_ref[...] = jnp.where(row < HEAD_DIM, q, 0.0).astype(BF16)
    qb_ref[...] = jnp.where(row >= HEAD_DIM, q, 0.0).astype(BF16)
    m_ref[...] = jnp.full(m_ref.shape, NEG_INF, F32)
    acc_ref[...] = jnp.zeros(acc_ref.shape, F32)
    c_left = scal_ref[h]
    c_right = scal_ref[DIFF_HEADS + h]
    ones_rows = jnp.where(lax.broadcasted_iota(jnp.int32, (DIFF_VROWS - two_d, tk), 0) == 0,
                          1.0, 0.0).astype(BF16)

    chunks = [(c, slice(j, j + DIFF_CHUNK)) for c in range(2) for j in range(0, tq, DIFF_CHUNK)]
    qm_refs = (qa_ref, qb_ref)
    even = (s0_ref, t0_ref)
    odd = (s1_ref, t1_ref)

    def scores(kb, bias_fn, bufs, c, cols):
        s_ref, t_ref = bufs[0], bufs[1]
        s = jnp.dot(kb, qm_refs[c][:, cols], preferred_element_type=F32)
        if bias_fn is not None:
            s = s + bias_fn(cols)
        s_ref[c, :, cols] = s
        t_ref[c, :, cols] = jnp.max(s, axis=0, keepdims=True)

    def attend(shift, ki, bufs, c, cols):
        s_ref, t_ref = bufs
        m_old = m_ref[c, :, cols]
        m_new = jnp.maximum(m_old, t_ref[c, :, cols] + shift)
        alpha = jnp.exp2(m_old - m_new)
        p = jnp.exp2(s_ref[c, :, cols] - (m_new - shift)).astype(BF16)
        vt = jnp.concatenate([v_ref[ki], ones_rows], axis=0)
        acc_ref[c, :, cols] = alpha * acc_ref[c, :, cols] + jnp.dot(vt, p, preferred_element_type=F32)
        m_ref[c, :, cols] = m_new

    def block(nxt, cur):
        for c, cols in chunks:
            if nxt is not None:
                kb = k_ref[pl.ds(pl.multiple_of(nxt[0] * tk, tk), tk), :]
                scores(kb, nxt[1], nxt[2], c, cols)
            if cur is not None:
                attend(cur[1], cur[0], cur[2], c, cols)

    has_left = qi >= 1
    has_right = qi + 1 < nk
    k_left = jnp.where(has_left, qi - 1, nk - 1)
    k_right = jnp.where(has_right, qi + 1, 0)

    def near_bias(index, exists, const):
        def fn(cols):
            bias = b_ref[0, index, :, cols]
            return bias if exists is None else jnp.where(exists, bias, const)
        return fn

    first_far = jnp.where(has_right, 0, 1)
    win_lo = jnp.maximum(qi - 1, 0)
    win_len = jnp.minimum(qi + 1, nk - 1) - win_lo + 1
    n_far = nk - 3

    def far(f):
        ki = first_far + f
        ki = jnp.minimum(jnp.where(ki < win_lo, ki, ki + win_len), nk - 1)
        return ki, jnp.where(ki < qi, c_left, c_right)

    far0, _ = far(0)
    block((qi, near_bias(1, None, None), even), None)
    block((k_left, near_bias(0, has_left, c_right), odd), (qi, 0.0, even))
    block((k_right, near_bias(2, has_right, c_left), even), (k_left, 0.0, odd))
    block((far0, None, odd), (k_right, 0.0, even))

    def pair(f):
        k_a, shift_a = far(f)
        k_b, shift_b = far(f + 1)
        k_c, _ = far(f + 2)
        block((k_b, None, even), (k_a, shift_a, odd))
        block((k_c, None, odd), (k_b, shift_b, even))

    n_pairs = (n_far - 1) // 2
    per_trip = max(d for d in range(1, DIFF_MAX_PAIRS_PER_TRIP + 1) if n_pairs % d == 0)

    def pairs(j, carry):
        for i in range(per_trip):
            pair(2 * (per_trip * j + i))
        return carry

    assert n_far % 2 == 1
    lax.fori_loop(0, n_pairs // per_trip, pairs, 0)
    k_last, shift_last = far(n_far - 1)
    block(None, (k_last, shift_last, odd))

    lam_init = scal_ref[2 * DIFF_HEADS]
    lq = lq_ref[...]
    lam = (jnp.exp(jnp.sum(lq[0:1] * lq[1:2], axis=-1, keepdims=True))
           - jnp.exp(jnp.sum(lq[2:3] * lq[3:4], axis=-1, keepdims=True)) + lam_init)
    acc0, acc1 = acc_ref[0], acc_ref[1]
    o = (acc0[:two_d] / acc0[two_d:two_d + 1]
         - lam * (acc1[:two_d] / acc1[two_d:two_d + 1]))
    y = o * lax.rsqrt(jnp.mean(o * o, axis=0, keepdims=True) + EPS) * g_ref[...]
    o_ref[...] = jnp.transpose(y * (1.0 - lam_init)).astype(o_ref.dtype)


def _diff(scal, proj, proj_t, bias_t, subln_g, lq):
    S = proj.shape[0]
    tq, tk = DIFF_TQ, DIFF_TK
    nk = S // tk
    assert nk % 2 == 0 and tq == ROW_TILE and tk == ROW_TILE
    two_d = 2 * HEAD_DIM
    kcol = N_KD // LANES
    qrow, vrow = T_QD // two_d, T_VD // two_d
    return pl.pallas_call(
        functools.partial(_diff_kernel, nk=nk),
        grid=(DIFF_HEADS, S // tq),
        in_specs=[
            pl.BlockSpec(memory_space=pltpu.SMEM),
            pl.BlockSpec((1, two_d, tq), lambda h, i: (i, qrow + h, 0)),
            pl.BlockSpec((S, LANES), lambda h, i: (0, kcol + h)),
            pl.BlockSpec((nk, two_d, tk), lambda h, i: (0, vrow + h, 0)),
            pl.BlockSpec((1, 3, tk, tq), lambda h, i: (h, 0, 0, 0)),
            _resident((two_d, 1), lambda h, i: (0, 0)),
            _resident((4, HEAD_DIM), lambda h, i: (0, 0)),
        ],
        out_specs=pl.BlockSpec((tq, two_d), lambda h, i: (i, h)),
        out_shape=jax.ShapeDtypeStruct((S, DIFF_HEADS * two_d), BF16),
        scratch_shapes=[
            pltpu.VMEM((two_d, tq), BF16),
            pltpu.VMEM((two_d, tq), BF16),
            pltpu.VMEM((2, tk, tq), F32),
            pltpu.VMEM((2, tk, tq), F32),
            pltpu.VMEM((2, 1, tq), F32),
            pltpu.VMEM((2, 1, tq), F32),
            pltpu.VMEM((2, 1, tq), F32),
            pltpu.VMEM((2, DIFF_VROWS, tq), F32),
        ],
        compiler_params=_params("parallel", "arbitrary"),
        name="diff_attn",
    )(scal, proj_t, proj, proj_t, bias_t, subln_g, lq)


def _diff_bias_tiles(table):
    tq, tk = DIFF_TQ, DIFF_TK
    tiles = []
    for d in (-1, 0, 1):
        vals = _rel_bias(table, d * tk - (tq - 1), d * tk + tk - 1)
        tiles.append(jnp.swapaxes(_toeplitz(vals, tq, tk), -1, -2))
    return jnp.stack(tiles, axis=1)


def _gqa_kernel(q_ref, kp_ref, kc_ref, kn_ref, vp_ref, vc_ref, vn_ref, b_ref, sink_ref, o_ref, *, n_blocks):
    blk = pl.program_id(1)
    sub = GQA_QB // WINDOW
    kw = 3 * WINDOW
    width = GQA_GROUP * HEAD_DIM
    qt = q_ref[0].astype(F32)
    kx = jnp.concatenate([kp_ref[...], kc_ref[...], kn_ref[...]], axis=0)
    vt = jnp.concatenate([vp_ref[0], vc_ref[0], vn_ref[0]], axis=1)
    row_head = lax.broadcasted_iota(jnp.int32, (width, WINDOW), 0) // HEAD_DIM
    krow = lax.broadcasted_iota(jnp.int32, (kw, 1), 0)
    ones_rows = jnp.where(lax.broadcasted_iota(jnp.int32, (BF16_ROWS, kw), 0) == 0, 1.0, 0.0).astype(BF16)
    sink = sink_ref[0]
    outs = [None] * sub

    def logits(i):
        qi = qt[:, i * WINDOW:(i + 1) * WINDOW]
        qs = jnp.concatenate([jnp.where(row_head == g, qi, 0.0).astype(BF16) for g in range(GQA_GROUP)], axis=1)
        s = jnp.dot(kx[i * WINDOW:i * WINDOW + kw], qs, preferred_element_type=F32) + b_ref[0]
        if i == 0:
            s = jnp.where((krow < WINDOW) & (blk == 0), NEG_INF, s)
        if i == sub - 1:
            s = jnp.where((krow >= 2 * WINDOW) & (blk == n_blocks - 1), NEG_INF, s)
        return s

    def attend(i, s):
        m = jnp.maximum(jnp.max(s, axis=0, keepdims=True), sink)
        p = jnp.exp2(s - m).astype(BF16)
        v_ext = jnp.concatenate([vt[:, i * WINDOW:i * WINDOW + kw], ones_rows], axis=0)
        o = jnp.dot(v_ext, p, preferred_element_type=F32)
        o = o[:HEAD_DIM] / (o[HEAD_DIM:HEAD_DIM + 1] + jnp.exp2(sink - m))
        outs[i] = jnp.concatenate([o[:, g * WINDOW:(g + 1) * WINDOW] for g in range(GQA_GROUP)], axis=0)

    _staggered(sub, logits, attend)
    o_ref[...] = jnp.transpose(jnp.concatenate(outs, axis=1)).astype(o_ref.dtype)


def _gqa(proj, proj_t, bias_t, sink_lanes):
    S = proj.shape[0]
    qb, kw = GQA_QB, 3 * WINDOW
    assert qb == ROW_TILE
    width = GQA_GROUP * HEAD_DIM
    nb = S // qb
    sub = qb // WINDOW
    last = S // WINDOW - 1
    kcol = N_KC // width
    qrow = T_QC // width
    vrow = T_VC // HEAD_DIM

    def prev_blk(b):
        return jnp.maximum(b * sub - 1, 0)

    def next_blk(b):
        return jnp.minimum(b * sub + sub, last)

    return pl.pallas_call(
        functools.partial(_gqa_kernel, n_blocks=nb),
        grid=(GQA_KV_HEADS, nb),
        in_specs=[
            pl.BlockSpec((1, width, qb), lambda kv, b: (b, qrow + kv, 0)),
            pl.BlockSpec((WINDOW, width), lambda kv, b: (prev_blk(b), kcol + kv)),
            pl.BlockSpec((qb, width), lambda kv, b: (b, kcol + kv)),
            pl.BlockSpec((WINDOW, width), lambda kv, b: (next_blk(b), kcol + kv)),
            pl.BlockSpec((1, HEAD_DIM, WINDOW), lambda kv, b: (prev_blk(b) // sub, vrow + kv, prev_blk(b) % sub)),
            pl.BlockSpec((1, HEAD_DIM, qb), lambda kv, b: (b, vrow + kv, 0)),
            pl.BlockSpec((1, HEAD_DIM, WINDOW), lambda kv, b: (next_blk(b) // sub, vrow + kv, next_blk(b) % sub)),
            pl.BlockSpec((1, kw, GQA_GROUP * WINDOW), lambda kv, b: (kv, 0, 0)),
            pl.BlockSpec((1, 1, GQA_GROUP * WINDOW), lambda kv, b: (kv, 0, 0)),
        ],
        out_specs=pl.BlockSpec((qb, width), lambda kv, b: (b, kv)),
        out_shape=jax.ShapeDtypeStruct((S, GQA_Q_HEADS * HEAD_DIM), BF16),
        compiler_params=_params("parallel", "arbitrary"),
        name="gqa_attn",
    )(proj_t, proj, proj, proj, proj_t, proj_t, proj_t, bias_t, sink_lanes)


def _gqa_bias(table):
    qb, kw = WINDOW, 3 * WINDOW
    vals = _rel_bias(table, -WINDOW - (qb - 1), kw - 1 - WINDOW)
    rel = jnp.arange(-WINDOW - (qb - 1), kw - WINDOW, dtype=jnp.int32)
    vals = jnp.where(jnp.abs(rel)[None, :] <= WINDOW, vals, NEG_INF)
    bias = _toeplitz(vals, qb, kw)
    bias = bias.reshape(GQA_KV_HEADS, GQA_GROUP, qb, kw)
    return jnp.transpose(bias, (0, 3, 1, 2)).reshape(GQA_KV_HEADS, kw, GQA_GROUP * qb)


def _merge_kernel(x_ref, ya_ref, yb_ref, yc_ref, g_ref, wg_ref, bg_ref, wb_ref, wo_ref, o_ref):
    x = x_ref[...]
    h = _rms(x, g_ref[...]).astype(BF16)
    gates = _sigmoid(jnp.dot(h, wg_ref[...], preferred_element_type=F32) + bg_ref[...])
    merged = None
    for i, y_ref in enumerate((ya_ref, yb_ref, yc_ref)):
        t = gates[:, i * D_MODEL:(i + 1) * D_MODEL] * jnp.dot(y_ref[...], wb_ref[i], preferred_element_type=F32)
        merged = t if merged is None else merged + t
    o_ref[...] = x + jnp.dot(merged.astype(BF16), wo_ref[...], preferred_element_type=F32)


def _merge(x, ya, yb, yc, g, wg, bg, wb, wo, layer):
    S = x.shape[0]
    tm = ROW_TILE
    branch = pl.BlockSpec((tm, BRANCH_WIDTH), lambda i: (i, 0))
    return pl.pallas_call(
        _merge_kernel,
        grid=(S // tm,),
        in_specs=[
            pl.BlockSpec((tm, D_MODEL), lambda i: (i, 0)),
            branch, branch, branch,
            _resident((1, D_MODEL), lambda i: (0, 0)),
            _resident((None, D_MODEL, N_BRANCHES * D_MODEL), lambda i: (layer, 0, 0)),
            _resident((1, N_BRANCHES * D_MODEL), lambda i: (0, 0)),
            _resident((None, N_BRANCHES, BRANCH_WIDTH, D_MODEL), lambda i: (layer, 0, 0, 0)),
            _resident((None, D_MODEL, D_MODEL), lambda i: (layer, 0, 0)),
        ],
        out_specs=pl.BlockSpec((tm, D_MODEL), lambda i: (i, 0)),
        out_shape=jax.ShapeDtypeStruct((S, D_MODEL), F32),
        compiler_params=_params("parallel"),
        name="merge",
    )(x, ya, yb, yc, g, wg, bg, wb, wo)


def kernel(x, w_in, w_branch, w_gate, b_gate, w_o, norm_g, final_g, ffn_w_gate, ffn_w_up, ffn_w_down,
           na_rpb, diff_lambda, diff_subln_g, gqa_sink, rel_bias_table):
    B, S, _ = x.shape
    assert B == 1 and S % (2 * DIFF_TK) == 0 and S // (NA_BLOCK_ROWS * GRID_W) >= 4
    depth = w_in.shape[0]
    two_d = 2 * HEAD_DIM

    col_scale = np.ones((W_IN_COLS,), np.float32)
    for off in (QA_OFF, QD_OFF, QC_OFF):
        col_scale[off:off + KA_OFF - QA_OFF] = HEAD_DIM ** -0.5 * LOG2E
    w_in_s = (w_in * col_scale).astype(BF16)
    w_kc_tiled = jnp.tile(w_in_s[:, :, KC_OFF:VC_OFF].reshape(depth, D_MODEL, GQA_KV_HEADS, 1, HEAD_DIM),
                          (1, 1, 1, GQA_GROUP, 1)).reshape(depth, D_MODEL, GQA_Q_HEADS * HEAD_DIM)
    w_in_n = jnp.concatenate([w_in_s[:, :, :QD_OFF], w_in_s[:, :, KD_OFF:VD_OFF], w_kc_tiled], axis=2)
    w_in_t = jnp.swapaxes(jnp.concatenate(
        [w_in_s[:, :, QD_OFF:KD_OFF], w_in_s[:, :, VD_OFF:QC_OFF], w_in_s[:, :, QC_OFF:KC_OFF],
         w_in_s[:, :, VC_OFF:]], axis=2), 1, 2)
    w_branch_b = w_branch.astype(BF16)
    w_gate_b = w_gate.astype(BF16)
    w_o_b = w_o.astype(BF16)
    ffn_wg_b = ffn_w_gate.astype(BF16)
    ffn_wu_b = ffn_w_up.astype(BF16)
    ffn_wd_b = ffn_w_down.astype(BF16)

    na_bias = _na_bias_tables(na_rpb * LOG2E, S // GRID_W)
    diff_table = rel_bias_table[:, :DIFF_HEADS]
    diff_bias = _diff_bias_tiles(diff_table) * LOG2E
    far = _rel_bias(diff_table, -T5_MAX_DIST, T5_MAX_DIST) * LOG2E
    gqa_bias = _gqa_bias(rel_bias_table[:, DIFF_HEADS:]) * LOG2E
    final_row = final_g.reshape(1, D_MODEL).astype(F32)

    xs = x.reshape(S, D_MODEL)
    for l in range(depth):
        xs = _ffn(xs, norm_g[l, 0].reshape(1, D_MODEL), ffn_wg_b, ffn_wu_b, ffn_wd_b, final_row, l, 0, False)
        g1 = norm_g[l, 1].reshape(1, D_MODEL)
        proj, proj_t = _proj(xs, g1, w_in_n, w_in_t, l)

        ya = _na(proj, na_bias, l)

        lam_init = 0.8 - 0.6 * math.exp(-0.3 * l)
        scal = jnp.concatenate([far[:, 0], far[:, -1], jnp.full((1,), lam_init, F32)])
        yb = _diff(scal, proj, proj_t, diff_bias, diff_subln_g[l].reshape(two_d, 1).astype(F32),
                   diff_lambda[l].astype(F32))

        sink_lanes = jnp.repeat(gqa_sink[l].astype(F32) * LOG2E, WINDOW).reshape(
            GQA_KV_HEADS, 1, GQA_GROUP * WINDOW)
        yc = _gqa(proj, proj_t, gqa_bias, sink_lanes)

        xs = _merge(xs, ya, yb, yc, g1, w_gate_b, b_gate[l].reshape(1, -1).astype(F32), w_branch_b, w_o_b, l)
        xs = _ffn(xs, norm_g[l, 2].reshape(1, D_MODEL), ffn_wg_b, ffn_wu_b, ffn_wd_b, final_row, l, 1,
                  l == depth - 1)
    return xs.reshape(B, S, D_MODEL)
```

```python
import functools
import math

import jax
import jax.numpy as jnp
import numpy as np
from jax import lax
from jax.experimental import pallas as pl
from jax.experimental.pallas import tpu as pltpu

D_MODEL = 1024
HEAD_DIM = 64
NA_HEADS = 8
DIFF_HEADS = 4
GQA_Q_HEADS = 8
GQA_KV_HEADS = 2
GQA_GROUP = GQA_Q_HEADS // GQA_KV_HEADS
BRANCH_WIDTH = 512
N_BRANCHES = 3
D_FF = 2816
GRID_W = 64
NA_WIN_ROWS = 8
NA_WIN_COLS = 16
WINDOW = 128
T5_BUCKETS = 32
T5_MAX_DIST = 128
NEG_INF = -1e30
EPS = 1e-6

QA_OFF, KA_OFF, VA_OFF = 0, 512, 1024
QD_OFF, KD_OFF, VD_OFF = 1536, 2048, 2560
QC_OFF, KC_OFF, VC_OFF = 3072, 3584, 3712
W_IN_COLS = 3840
N_QA, N_KA, N_VA, N_KD, N_KC = 0, 512, 1024, 1536, 2048
PROJ_N_COLS = 2560
T_QD, T_VD, T_QC, T_VC = 0, 512, 1024, 1536
PROJ_T_ROWS = 1664

LANES = 128
BF16_ROWS = 16
VMEM_LIMIT = 56 * 1024 * 1024

ROW_TILE = 512
NA_BLOCK_ROWS = 4
NA_WIN_BLOCK_ROWS = 12
DIFF_TQ = 512
DIFF_TK = 512
DIFF_CHUNK = 256
DIFF_MAX_PAIRS_PER_TRIP = 7
DIFF_VROWS = 2 * HEAD_DIM + BF16_ROWS
LOG2E = math.log2(math.e)
GQA_QB = 512

F32 = jnp.float32
BF16 = jnp.bfloat16


def _resident(shape, index_map):
    return pl.BlockSpec(shape, index_map, pipeline_mode=pl.Buffered(1))


def _params(*sem):
    return pltpu.CompilerParams(dimension_semantics=sem, vmem_limit_bytes=VMEM_LIMIT)


def _rms(x, g):
    return x * lax.rsqrt(jnp.mean(x * x, axis=-1, keepdims=True) + EPS) * g


def _sigmoid(z):
    return 1.0 / (1.0 + jnp.exp(-z))


def _ffn_kernel(x_ref, g_ref, wg_ref, wu_ref, wd_ref, fg_ref, o_ref, *, final):
    x = x_ref[...]
    h = _rms(x, g_ref[...]).astype(BF16)
    gate = jnp.dot(h, wg_ref[...], preferred_element_type=F32)
    up = jnp.dot(h, wu_ref[...], preferred_element_type=F32)
    a = (gate * _sigmoid(gate) * up).astype(BF16)
    y = x + 0.5 * jnp.dot(a, wd_ref[...], preferred_element_type=F32)
    if final:
        y = _rms(y, fg_ref[...])
    o_ref[...] = y


def _ffn(x, g, wg, wu, wd, fg, layer, which, final):
    S = x.shape[0]
    tm = ROW_TILE
    pick = lambda i: (layer, which, 0, 0)
    return pl.pallas_call(
        functools.partial(_ffn_kernel, final=final),
        grid=(S // tm,),
        in_specs=[
            pl.BlockSpec((tm, D_MODEL), lambda i: (i, 0)),
            _resident((1, D_MODEL), lambda i: (0, 0)),
            _resident((None, None, D_MODEL, D_FF), pick),
            _resident((None, None, D_MODEL, D_FF), pick),
            _resident((None, None, D_FF, D_MODEL), pick),
            _resident((1, D_MODEL), lambda i: (0, 0)),
        ],
        out_specs=pl.BlockSpec((tm, D_MODEL), lambda i: (i, 0)),
        out_shape=jax.ShapeDtypeStruct((S, D_MODEL), F32),
        compiler_params=_params("parallel"),
        name="ffn",
    )(x, g, wg, wu, wd, fg)


def _proj_kernel(x_ref, g_ref, wn_ref, wt_ref, on_ref, ot_ref):
    h = _rms(x_ref[...], g_ref[...]).astype(BF16)
    on_ref[...] = jnp.dot(h, wn_ref[...], preferred_element_type=F32).astype(BF16)
    ot_ref[0] = lax.dot_general(wt_ref[...], h, (((1,), (1,)), ((), ())),
                                preferred_element_type=F32).astype(BF16)


def _proj(x, g, wn, wt, layer):
    S = x.shape[0]
    tm = ROW_TILE
    return pl.pallas_call(
        _proj_kernel,
        grid=(S // tm,),
        in_specs=[
            pl.BlockSpec((tm, D_MODEL), lambda i: (i, 0)),
            _resident((1, D_MODEL), lambda i: (0, 0)),
            _resident((None, D_MODEL, PROJ_N_COLS), lambda i: (layer, 0, 0)),
            _resident((None, PROJ_T_ROWS, D_MODEL), lambda i: (layer, 0, 0)),
        ],
        out_specs=[
            pl.BlockSpec((tm, PROJ_N_COLS), lambda i: (i, 0)),
            pl.BlockSpec((1, PROJ_T_ROWS, tm), lambda i: (i, 0, 0)),
        ],
        out_shape=[
            jax.ShapeDtypeStruct((S, PROJ_N_COLS), BF16),
            jax.ShapeDtypeStruct((S // tm, PROJ_T_ROWS, tm), BF16),
        ],
        compiler_params=_params("parallel"),
        name="proj",
    )(x, g, wn, wt)


def _staggered(n, stage_a, stage_b):
    ahead = [stage_a(i) for i in range(min(2, n))]
    for i in range(n):
        if i + 2 < n:
            ahead.append(stage_a(i + 2))
        stage_b(i, ahead[i])


def _na_kernel(q_ref, k0_ref, k1_ref, k2_ref, v0_ref, v1_ref, v2_ref, b_ref, o_ref):
    nq = NA_BLOCK_ROWS * GRID_W
    q = q_ref[...].astype(F32)
    k3 = jnp.concatenate([k0_ref[...], k1_ref[...], k2_ref[...]], axis=0)
    v3 = jnp.concatenate([v0_ref[...], v1_ref[...], v2_ref[...]], axis=0)
    first = lax.broadcasted_iota(jnp.int32, (nq, LANES), 1) < HEAD_DIM
    outs = [None] * NA_HEADS

    def logits(head):
        cols = slice((head // 2) * LANES, (head // 2 + 1) * LANES)
        keep = first if head % 2 == 0 else jnp.logical_not(first)
        qm = jnp.where(keep, q[:, cols], 0.0).astype(BF16)
        s = lax.dot_general(qm, k3[:, cols], (((1,), (1,)), ((), ())), preferred_element_type=F32)
        return s + b_ref[0, head]

    def attend(head, s):
        cols = slice((head // 2) * LANES, (head // 2 + 1) * LANES)
        m = jnp.max(s, axis=-1, keepdims=True)
        p = jnp.exp2(s - m)
        l = jnp.sum(p, axis=-1, keepdims=True)
        outs[head] = jnp.dot(p.astype(BF16), v3[:, cols], preferred_element_type=F32) / l

    _staggered(NA_HEADS, logits, attend)
    for pair in range(NA_HEADS // 2):
        o_ref[:, pair * LANES:(pair + 1) * LANES] = jnp.where(
            first, outs[2 * pair], outs[2 * pair + 1]).astype(o_ref.dtype)


def _na(proj, bias, layer):
    S = proj.shape[0]
    nq = NA_BLOCK_ROWS * GRID_W
    nb = S // nq
    width = NA_HEADS * HEAD_DIM
    qcol, kcol, vcol = N_QA // width, N_KA // width, N_VA // width

    def window(col, j):
        return pl.BlockSpec((nq, width), lambda b: (jnp.clip(b - 1, 0, nb - 3) + j, col))

    return pl.pallas_call(
        _na_kernel,
        grid=(nb,),
        in_specs=[
            pl.BlockSpec((nq, width), lambda b: (b, qcol)),
            window(kcol, 0), window(kcol, 1), window(kcol, 2),
            window(vcol, 0), window(vcol, 1), window(vcol, 2),
            pl.BlockSpec((None, 1, NA_HEADS, nq, 3 * nq),
                         lambda b: (layer, jnp.where(b == 0, 0, jnp.where(b == nb - 1, 2, 1)), 0, 0, 0)),
        ],
        out_specs=pl.BlockSpec((nq, width), lambda b: (b, 0)),
        out_shape=jax.ShapeDtypeStruct((S, width), BF16),
        compiler_params=_params("parallel"),
        name="na_attn",
    )(proj, proj, proj, proj, proj, proj, proj, bias)


def _na_bias_tables(rpb, rows):
    br, wr = NA_BLOCK_ROWS, NA_WIN_BLOCK_ROWS
    half = NA_WIN_ROWS // 2
    c = np.arange(GRID_W)
    cs = np.clip(c - NA_WIN_COLS // 2, 0, GRID_W - NA_WIN_COLS)
    col_ok = (c[None, :] >= cs[:, None]) & (c[None, :] < cs[:, None] + NA_WIN_COLS)
    pad = GRID_W - NA_WIN_COLS
    vals = jnp.pad(rpb.astype(F32), ((0, 0), (0, 0), (0, 0), (pad, pad)))
    t_all = jnp.where(col_ok, _toeplitz(vals, GRID_W, GRID_W), NEG_INF)
    L, H = rpb.shape[0], rpb.shape[1]
    n_dr = 2 * NA_WIN_ROWS - 1
    seq = jnp.moveaxis(t_all, 2, 3).reshape(L, H, GRID_W, n_dr * GRID_W)
    seq = jnp.pad(seq, ((0, 0), (0, 0), (0, 0), (wr * GRID_W, wr * GRID_W)), constant_values=NEG_INF)
    dr_of_lane = np.arange((n_dr + 2 * wr) * GRID_W) // GRID_W - wr
    types = []
    for b in (0, 1, rows // br - 1):
        win0 = int(np.clip(b * br - half, 0, rows - wr))
        row_blocks = []
        for j in range(br):
            r = b * br + j
            rs = int(np.clip(r - half, 0, rows - NA_WIN_ROWS))
            w_lo, dr_lo = rs - win0, rs - r + (NA_WIN_ROWS - 1)
            inside = (dr_of_lane >= dr_lo) & (dr_of_lane < dr_lo + NA_WIN_ROWS)
            start = (wr + dr_lo - w_lo) * GRID_W
            row_blocks.append(jnp.where(inside, seq, NEG_INF)[..., start:start + wr * GRID_W])
        types.append(jnp.concatenate(row_blocks, axis=2))
    return jnp.stack(types, axis=1)


def _t5_bucket(rel):
    half = T5_BUCKETS // 2
    max_exact = half // 2
    ret = (rel > 0).astype(jnp.int32) * half
    n = jnp.abs(rel)
    nf = jnp.maximum(n, 1).astype(jnp.float32)
    large = max_exact + (jnp.log(nf / max_exact) / math.log(T5_MAX_DIST / max_exact)
                         * (half - max_exact)).astype(jnp.int32)
    large = jnp.minimum(large, half - 1)
    return ret + jnp.where(n < max_exact, n, large)


def _toeplitz(vals, n_rows, n_cols):
    length = n_rows + n_cols - 1
    lead = vals.shape[:-1]
    w = jnp.concatenate([vals, jnp.zeros(lead + (1,), vals.dtype)], axis=-1)
    flat = jnp.tile(w, (1,) * len(lead) + (n_rows,))[..., : n_rows * length]
    return flat.reshape(lead + (n_rows, length))[..., n_rows - 1:]


def _rel_bias(table, lo, hi):
    rel = jnp.arange(lo, hi + 1, dtype=jnp.int32)
    return jnp.transpose(table.astype(F32)[_t5_bucket(rel)], (1, 0))


def _diff_kernel(scal_ref, q_ref, k_ref, v_ref, b_ref, g_ref, lq_ref, o_ref,
                 qa_ref, qb_ref, s0_ref, s1_ref, t0_ref, t1_ref,
                 m_ref, acc_ref, *, nk):
    h = pl.program_id(0)
    qi = pl.program_id(1)
    tq, tk = DIFF_TQ, DIFF_TK
    two_d = 2 * HEAD_DIM

    q = q_ref[0].astype(F32)
    row = lax.broadcasted_iota(jnp.int32, (two_d, tq), 0)
    qa_ref[...] = jnp.where(row < HEAD_DIM, q, 0.0).astype(BF16)
    qb_ref[...] = jnp.where(row >= HEAD_DIM, q, 0.0).astype(BF16)
    m_ref[...] = jnp.full(m_ref.shape, NEG_INF, F32)
    acc_ref[...] = jnp.zeros(acc_ref.shape, F32)
    c_left = scal_ref[h]
    c_right = scal_ref[DIFF_HEADS + h]
    ones_rows = jnp.where(lax.broadcasted_iota(jnp.int32, (DIFF_VROWS - two_d, tk), 0) == 0,
                          1.0, 0.0).astype(BF16)

    chunks = [(c, slice(j, j + DIFF_CHUNK)) for c in range(2) for j in range(0, tq, DIFF_CHUNK)]
    qm_refs = (qa_ref, qb_ref)
    even = (s0_ref, t0_ref)
    odd = (s1_ref, t1_ref)

    def scores(kb, bias_fn, bufs, c, cols):
        s_ref, t_ref = bufs[0], bufs[1]
        s = jnp.dot(kb, qm_refs[c][:, cols], preferred_element_type=F32)
        if bias_fn is not None:
            s = s + bias_fn(cols)
        s_ref[c, :, cols] = s
        t_ref[c, :, cols] = jnp.max(s, axis=0, keepdims=True)

    def attend(shift, ki, bufs, c, cols):
        s_ref, t_ref = bufs
        m_old = m_ref[c, :, cols]
        m_new = jnp.maximum(m_old, t_ref[c, :, cols] + shift)
        alpha = jnp.exp2(m_old - m_new)
        p = jnp.exp2(s_ref[c, :, cols] - (m_new - shift)).astype(BF16)
        vt = jnp.concatenate([v_ref[ki], ones_rows], axis=0)
        acc_ref[c, :, cols] = alpha * acc_ref[c, :, cols] + jnp.dot(vt, p, preferred_element_type=F32)
        m_ref[c, :, cols] = m_new

    def block(nxt, cur):
        for c, cols in chunks:
            if nxt is not None:
                kb = k_ref[pl.ds(pl.multiple_of(nxt[0] * tk, tk), tk), :]
                scores(kb, nxt[1], nxt[2], c, cols)
            if cur is not None:
                attend(cur[1], cur[0], cur[2], c, cols)

    has_left = qi >= 1
    has_right = qi + 1 < nk
    k_left = jnp.where(has_left, qi - 1, nk - 1)
    k_right = jnp.where(has_right, qi + 1, 0)

    def near_bias(index, exists, const):
        def fn(cols):
            bias = b_ref[0, index, :, cols]
            return bias if exists is None else jnp.where(exists, bias, const)
        return fn

    first_far = jnp.where(has_right, 0, 1)
    win_lo = jnp.maximum(qi - 1, 0)
    win_len = jnp.minimum(qi + 1, nk - 1) - win_lo + 1
    n_far = nk - 3

    def far(f):
        ki = first_far + f
        ki = jnp.minimum(jnp.where(ki < win_lo, ki, ki + win_len), nk - 1)
        return ki, jnp.where(ki < qi, c_left, c_right)

    far0, _ = far(0)
    block((qi, near_bias(1, None, None), even), None)
    block((k_left, near_bias(0, has_left, c_right), odd), (qi, 0.0, even))
    block((k_right, near_bias(2, has_right, c_left), even), (k_left, 0.0, odd))
    block((far0, None, odd), (k_right, 0.0, even))

    def pair(f):
        k_a, shift_a = far(f)
        k_b, shift_b = far(f + 1)
        k_c, _ = far(f + 2)
        block((k_b, None, even), (k_a, shift_a, odd))
        block((k_c, None, odd), (k_b, shift_b, even))

    n_pairs = (n_far - 1) // 2
    per_trip = max(d for d in range(1, DIFF_MAX_PAIRS_PER_TRIP + 1) if n_pairs % d == 0)

    def pairs(j, carry):
        for i in range(per_trip):
            pair(2 * (per_trip * j + i))
        return carry

    assert n_far % 2 == 1
    lax.fori_loop(0, n_pairs // per_trip, pairs, 0)
    k_last, shift_last = far(n_far - 1)
    block(None, (k_last, shift_last, odd))

    lam_init = scal_ref[2 * DIFF_HEADS]
    lq = lq_ref[...]
    lam = (jnp.exp(jnp.sum(lq[0:1] * lq[1:2], axis=-1, keepdims=True))
           - jnp.exp(jnp.sum(lq[2:3] * lq[3:4], axis=-1, keepdims=True)) + lam_init)
    acc0, acc1 = acc_ref[0], acc_ref[1]
    o = (acc0[:two_d] / acc0[two_d:two_d + 1]
         - lam * (acc1[:two_d] / acc1[two_d:two_d + 1]))
    y = o * lax.rsqrt(jnp.mean(o * o, axis=0, keepdims=True) + EPS) * g_ref[...]
    o_ref[...] = jnp.transpose(y * (1.0 - lam_init)).astype(o_ref.dtype)


def _diff(scal, proj, proj_t, bias_t, subln_g, lq):
    S = proj.shape[0]
    tq, tk = DIFF_TQ, DIFF_TK
    nk = S // tk
    assert nk % 2 == 0 and tq == ROW_TILE and tk == ROW_TILE
    two_d = 2 * HEAD_DIM
    kcol = N_KD // LANES
    qrow, vrow = T_QD // two_d, T_VD // two_d
    return pl.pallas_call(
        functools.partial(_diff_kernel, nk=nk),
        grid=(DIFF_HEADS, S // tq),
        in_specs=[
            pl.BlockSpec(memory_space=pltpu.SMEM),
            pl.BlockSpec((1, two_d, tq), lambda h, i: (i, qrow + h, 0)),
            pl.BlockSpec((S, LANES), lambda h, i: (0, kcol + h)),
            pl.BlockSpec((nk, two_d, tk), lambda h, i: (0, vrow + h, 0)),
            pl.BlockSpec((1, 3, tk, tq), lambda h, i: (h, 0, 0, 0)),
            _resident((two_d, 1), lambda h, i: (0, 0)),
            _resident((4, HEAD_DIM), lambda h, i: (0, 0)),
        ],
        out_specs=pl.BlockSpec((tq, two_d), lambda h, i: (i, h)),
        out_shape=jax.ShapeDtypeStruct((S, DIFF_HEADS * two_d), BF16),
        scratch_shapes=[
            pltpu.VMEM((two_d, tq), BF16),
            pltpu.VMEM((two_d, tq), BF16),
            pltpu.VMEM((2, tk, tq), F32),
            pltpu.VMEM((2, tk, tq), F32),
            pltpu.VMEM((2, 1, tq), F32),
            pltpu.VMEM((2, 1, tq), F32),
            pltpu.VMEM((2, 1, tq), F32),
            pltpu.VMEM((2, DIFF_VROWS, tq), F32),
        ],
        compiler_params=_params("parallel", "arbitrary"),
        name="diff_attn",
    )(scal, proj_t, proj, proj_t, bias_t, subln_g, lq)


def _diff_bias_tiles(table):
    tq, tk = DIFF_TQ, DIFF_TK
    tiles = []
    for d in (-1, 0, 1):
        vals = _rel_bias(table, d * tk - (tq - 1), d * tk + tk - 1)
        tiles.append(jnp.swapaxes(_toeplitz(vals, tq, tk), -1, -2))
    return jnp.stack(tiles, axis=1)


def _gqa_kernel(q_ref, kp_ref, kc_ref, kn_ref, vp_ref, vc_ref, vn_ref, b_ref, sink_ref, o_ref, *, n_blocks):
    blk = pl.program_id(1)
    sub = GQA_QB // WINDOW
    kw = 3 * WINDOW
    width = GQA_GROUP * HEAD_DIM
    qt = q_ref[0].astype(F32)
    kx = jnp.concatenate([kp_ref[...], kc_ref[...], kn_ref[...]], axis=0)
    vt = jnp.concatenate([vp_ref[0], vc_ref[0], vn_ref[0]], axis=1)
    row_head = lax.broadcasted_iota(jnp.int32, (width, WINDOW), 0) // HEAD_DIM
    krow = lax.broadcasted_iota(jnp.int32, (kw, 1), 0)
    ones_rows = jnp.where(lax.broadcasted_iota(jnp.int32, (BF16_ROWS, kw), 0) == 0, 1.0, 0.0).astype(BF16)
    sink = sink_ref[0]
    outs = [None] * sub

    def logits(i):
        qi = qt[:, i * WINDOW:(i + 1) * WINDOW]
        qs = jnp.concatenate([jnp.where(row_head == g, qi, 0.0).astype(BF16) for g in range(GQA_GROUP)], axis=1)
        s = jnp.dot(kx[i * WINDOW:i * WINDOW + kw], qs, preferred_element_type=F32) + b_ref[0]
        if i == 0:
            s = jnp.where((krow < WINDOW) & (blk == 0), NEG_INF, s)
        if i == sub - 1:
            s = jnp.where((krow >= 2 * WINDOW) & (blk == n_blocks - 1), NEG_INF, s)
        return s

    def attend(i, s):
        m = jnp.maximum(jnp.max(s, axis=0, keepdims=True), sink)
        p = jnp.exp2(s - m).astype(BF16)
        v_ext = jnp.concatenate([vt[:, i * WINDOW:i * WINDOW + kw], ones_rows], axis=0)
        o = jnp.dot(v_ext, p, preferred_element_type=F32)
        o = o[:HEAD_DIM] / (o[HEAD_DIM:HEAD_DIM + 1] + jnp.exp2(sink - m))
        outs[i] = jnp.concatenate([o[:, g * WINDOW:(g + 1) * WINDOW] for g in range(GQA_GROUP)], axis=0)

    _staggered(sub, logits, attend)
    o_ref[...] = jnp.transpose(jnp.concatenate(outs, axis=1)).astype(o_ref.dtype)


def _gqa(proj, proj_t, bias_t, sink_lanes):
    S = proj.shape[0]
    qb, kw = GQA_QB, 3 * WINDOW
    assert qb == ROW_TILE
    width = GQA_GROUP * HEAD_DIM
    nb = S // qb
    sub = qb // WINDOW
    last = S // WINDOW - 1
    kcol = N_KC // width
    qrow = T_QC // width
    vrow = T_VC // HEAD_DIM

    def prev_blk(b):
        return jnp.maximum(b * sub - 1, 0)

    def next_blk(b):
        return jnp.minimum(b * sub + sub, last)

    return pl.pallas_call(
        functools.partial(_gqa_kernel, n_blocks=nb),
        grid=(GQA_KV_HEADS, nb),
        in_specs=[
            pl.BlockSpec((1, width, qb), lambda kv, b: (b, qrow + kv, 0)),
            pl.BlockSpec((WINDOW, width), lambda kv, b: (prev_blk(b), kcol + kv)),
            pl.BlockSpec((qb, width), lambda kv, b: (b, kcol + kv)),
            pl.BlockSpec((WINDOW, width), lambda kv, b: (next_blk(b), kcol + kv)),
            pl.BlockSpec((1, HEAD_DIM, WINDOW), lambda kv, b: (prev_blk(b) // sub, vrow + kv, prev_blk(b) % sub)),
            pl.BlockSpec((1, HEAD_DIM, qb), lambda kv, b: (b, vrow + kv, 0)),
            pl.BlockSpec((1, HEAD_DIM, WINDOW), lambda kv, b: (next_blk(b) // sub, vrow + kv, next_blk(b) % sub)),
            pl.BlockSpec((1, kw, GQA_GROUP * WINDOW), lambda kv, b: (kv, 0, 0)),
            pl.BlockSpec((1, 1, GQA_GROUP * WINDOW), lambda kv, b: (kv, 0, 0)),
        ],
        out_specs=pl.BlockSpec((qb, width), lambda kv, b: (b, kv)),
        out_shape=jax.ShapeDtypeStruct((S, GQA_Q_HEADS * HEAD_DIM), BF16),
        compiler_params=_params("parallel", "arbitrary"),
        name="gqa_attn",
    )(proj_t, proj, proj, proj, proj_t, proj_t, proj_t, bias_t, sink_lanes)


def _gqa_bias(table):
    qb, kw = WINDOW, 3 * WINDOW
    vals = _rel_bias(table, -WINDOW - (qb - 1), kw - 1 - WINDOW)
    rel = jnp.arange(-WINDOW - (qb - 1), kw - WINDOW, dtype=jnp.int32)
    vals = jnp.where(jnp.abs(rel)[None, :] <= WINDOW, vals, NEG_INF)
    bias = _toeplitz(vals, qb, kw)
    bias = bias.reshape(GQA_KV_HEADS, GQA_GROUP, qb, kw)
    return jnp.transpose(bias, (0, 3, 1, 2)).reshape(GQA_KV_HEADS, kw, GQA_GROUP * qb)


def _merge_kernel(x_ref, ya_ref, yb_ref, yc_ref, g_ref, wg_ref, bg_ref, wb_ref, wo_ref, o_ref):
    x = x_ref[...]
    h = _rms(x, g_ref[...]).astype(BF16)
    gates = _sigmoid(jnp.dot(h, wg_ref[...], preferred_element_type=F32) + bg_ref[...])
    merged = None
    for i, y_ref in enumerate((ya_ref, yb_ref, yc_ref)):
        t = gates[:, i * D_MODEL:(i + 1) * D_MODEL] * jnp.dot(y_ref[...], wb_ref[i], preferred_element_type=F32)
        merged = t if merged is None else merged + t
    o_ref[...] = x + jnp.dot(merged.astype(BF16), wo_ref[...], preferred_element_type=F32)


def _merge(x, ya, yb, yc, g, wg, bg, wb, wo, layer):
    S = x.shape[0]
    tm = ROW_TILE
    branch = pl.BlockSpec((tm, BRANCH_WIDTH), lambda i: (i, 0))
    return pl.pallas_call(
        _merge_kernel,
        grid=(S // tm,),
        in_specs=[
            pl.BlockSpec((tm, D_MODEL), lambda i: (i, 0)),
            branch, branch, branch,
            _resident((1, D_MODEL), lambda i: (0, 0)),
            _resident((None, D_MODEL, N_BRANCHES * D_MODEL), lambda i: (layer, 0, 0)),
            _resident((1, N_BRANCHES * D_MODEL), lambda i: (0, 0)),
            _resident((None, N_BRANCHES, BRANCH_WIDTH, D_MODEL), lambda i: (layer, 0, 0, 0)),
            _resident((None, D_MODEL, D_MODEL), lambda i: (layer, 0, 0)),
        ],
        out_specs=pl.BlockSpec((tm, D_MODEL), lambda i: (i, 0)),
        out_shape=jax.ShapeDtypeStruct((S, D_MODEL), F32),
        compiler_params=_params("parallel"),
        name="merge",
    )(x, ya, yb, yc, g, wg, bg, wb, wo)


def kernel(x, w_in, w_branch, w_gate, b_gate, w_o, norm_g, final_g, ffn_w_gate, ffn_w_up, ffn_w_down,
           na_rpb, diff_lambda, diff_subln_g, gqa_sink, rel_bias_table):
    B, S, _ = x.shape
    assert B == 1 and S % (2 * DIFF_TK) == 0 and S // (NA_BLOCK_ROWS * GRID_W) >= 4
    depth = w_in.shape[0]
    two_d = 2 * HEAD_DIM

    col_scale = np.ones((W_IN_COLS,), np.float32)
    for off in (QA_OFF, QD_OFF, QC_OFF):
        col_scale[off:off + KA_OFF - QA_OFF] = HEAD_DIM ** -0.5 * LOG2E
    w_in_s = (w_in * col_scale).astype(BF16)
    w_kc_tiled = jnp.tile(w_in_s[:, :, KC_OFF:VC_OFF].reshape(depth, D_MODEL, GQA_KV_HEADS, 1, HEAD_DIM),
                          (1, 1, 1, GQA_GROUP, 1)).reshape(depth, D_MODEL, GQA_Q_HEADS * HEAD_DIM)
    w_in_n = jnp.concatenate([w_in_s[:, :, :QD_OFF], w_in_s[:, :, KD_OFF:VD_OFF], w_kc_tiled], axis=2)
    w_in_t = jnp.swapaxes(jnp.concatenate(
        [w_in_s[:, :, QD_OFF:KD_OFF], w_in_s[:, :, VD_OFF:QC_OFF], w_in_s[:, :, QC_OFF:KC_OFF],
         w_in_s[:, :, VC_OFF:]], axis=2), 1, 2)
    w_branch_b = w_branch.astype(BF16)
    w_gate_b = w_gate.astype(BF16)
    w_o_b = w_o.astype(BF16)
    ffn_wg_b = ffn_w_gate.astype(BF16)
    ffn_wu_b = ffn_w_up.astype(BF16)
    ffn_wd_b = ffn_w_down.astype(BF16)

    na_bias = _na_bias_tables(na_rpb * LOG2E, S // GRID_W)
    diff_table = rel_bias_table[:, :DIFF_HEADS]
    diff_bias = _diff_bias_tiles(diff_table) * LOG2E
    far = _rel_bias(diff_table, -T5_MAX_DIST, T5_MAX_DIST) * LOG2E
    gqa_bias = _gqa_bias(rel_bias_table[:, DIFF_HEADS:]) * LOG2E
    final_row = final_g.reshape(1, D_MODEL).astype(F32)

    xs = x.reshape(S, D_MODEL)
    for l in range(depth):
        xs = _ffn(xs, norm_g[l, 0].reshape(1, D_MODEL), ffn_wg_b, ffn_wu_b, ffn_wd_b, final_row, l, 0, False)
        g1 = norm_g[l, 1].reshape(1, D_MODEL)
        proj, proj_t = _proj(xs, g1, w_in_n, w_in_t, l)

        ya = _na(proj, na_bias, l)

        lam_init = 0.8 - 0.6 * math.exp(-0.3 * l)
        scal = jnp.concatenate([far[:, 0], far[:, -1], jnp.full((1,), lam_init, F32)])
        yb = _diff(scal, proj, proj_t, diff_bias, diff_subln_g[l].reshape(two_d, 1).astype(F32),
                   diff_lambda[l].astype(F32))

        sink_lanes = jnp.repeat(gqa_sink[l].astype(F32) * LOG2E, WINDOW).reshape(
            GQA_KV_HEADS, 1, GQA_GROUP * WINDOW)
        yc = _gqa(proj, proj_t, gqa_bias, sink_lanes)

        xs = _merge(xs, ya, yb, yc, g1, w_gate_b, b_gate[l].reshape(1, -1).astype(F32), w_branch_b, w_o_b, l)
        xs = _ffn(xs, norm_g[l, 2].reshape(1, D_MODEL), ffn_wg_b, ffn_wu_b, ffn_wd_b, final_row, l, 1,
                  l == depth - 1)
    return xs.reshape(B, S, D_MODEL)
```
